```python
import math
import jax, jax.numpy as jnp
from jax import lax
import numpy as np

D_MODEL = 1024
BATCH = 32
SEQ = 2048
DEPTH = 2

N_MEM = 256
EPS = 1e-6
NEG = -1e30
F_MIN = 1e-12
ATT_HEADS = 8
ATT_HEAD_DIM = 64
ATT_W = ATT_HEADS * ATT_HEAD_DIM
DILATED_PATTERNS = ((128, 1), (512, 4), (2048, 16))
BLK = 128
HG_HEADS = 4
HG_HEAD_DIM = 128
HG_W = HG_HEADS * HG_HEAD_DIM
HG_CHUNK = 32
MIX_W = ATT_W + HG_W
IN_W = 3 * ATT_W + 4 * HG_W
MEM_HEADS = 4
MEM_HEAD_DIM = D_MODEL // MEM_HEADS
D_FF = 4 * D_MODEL

kernel_name = "hybrid_dilated_attn_hgrn2_block"


def rms_norm(x, g):
    xf = x.astype(jnp.float32)
    y = xf * lax.rsqrt(jnp.mean(xf * xf, axis=-1, keepdims=True) + EPS)
    return (y * g.astype(jnp.float32)).astype(x.dtype)


def dilated_branch(q, k, v, window, dilation, slopes):
    B_, S_, H_, E_ = q.shape
    steps = window // dilation
    L = S_ // dilation
    nb = -(-L // BLK)
    Lp = nb * BLK

    def split(t):
        t = t.reshape(B_, L, dilation, H_, E_).transpose(0, 2, 3, 1, 4)
        t = jnp.pad(t, ((0, 0), (0, 0), (0, 0), (0, Lp - L), (0, 0)))
        return t.reshape(B_, dilation, H_, nb, BLK, E_)

    def with_prev(t):
        prev = jnp.pad(t, ((0, 0), (0, 0), (0, 0), (1, 0), (0, 0), (0, 0)))[:, :, :, :-1]
        return jnp.concatenate([prev, t], axis=-2)

    qb = split(q)
    kx = with_prev(split(k))
    vx = with_prev(split(v))
    s = jnp.einsum('bdhnqe,bdhnke->bdhnqk', qb, kx) * (1.0 / math.sqrt(E_))
    qi = jnp.arange(BLK)[:, None]
    kj = jnp.arange(2 * BLK)[None, :]
    step = qi + BLK - kj
    blk = jnp.arange(nb)[:, None, None]
    valid = (step >= 0) & (step <= steps) & (blk * BLK + kj - BLK >= 0)
    bias = -slopes[:, None, None, None] * (step * dilation).astype(jnp.float32)[None, None]
    s = jnp.where(valid, s + bias, NEG)
    lse = jax.nn.logsumexp(s, axis=-1)
    p = jnp.where(valid, jnp.exp(s - lse[..., None]), 0.0)
    o = jnp.einsum('bdhnqk,bdhnke->bdhnqe', p, vx)
    o = o.reshape(B_, dilation, H_, Lp, E_)[:, :, :, :L].transpose(0, 3, 1, 2, 4).reshape(B_, S_, H_, E_)
    lse = lse.reshape(B_, dilation, H_, Lp)[..., :L].transpose(0, 3, 1, 2).reshape(B_, S_, H_)
    return o, lse


def hgrn2_recurrence(q, f_logit, i, lb):
    B_, S_, H_, K_ = q.shape
    q = jax.nn.silu(q.astype(jnp.float32))
    i = i.astype(jnp.float32)
    fl = f_logit.astype(jnp.float32)
    f = lb + (1.0 - lb) * jax.nn.sigmoid(fl)
    log_f = jnp.log(jnp.maximum(f, F_MIN))
    k = (1.0 - lb) * jax.nn.sigmoid(-fl)
    n = S_ // HG_CHUNK

    def chunks(t):
        return t.reshape(B_, n, HG_CHUNK, H_, K_).transpose(1, 0, 3, 2, 4)

    qc, kc, vc, gc = chunks(q), chunks(k), chunks(i), chunks(log_f)
    bc = jnp.cumsum(gc, axis=-2)
    causal = jnp.tril(jnp.ones((HG_CHUNK, HG_CHUNK), dtype=bool))[:, :, None]

    def step(state, inp):
        qt, kt, vt, bt = inp
        diff = bt[..., :, None, :] - bt[..., None, :, :]
        decay = jnp.where(causal, jnp.exp(jnp.where(causal, diff, 0.0)), 0.0)
        scores = jnp.einsum('bhtk,bhsk,bhtsk->bhts', qt, kt, decay)
        o = jnp.einsum('bhts,bhsv->bhtv', scores, vt) + jnp.einsum('bhtk,bhkv->bhtv', qt * jnp.exp(bt), state)
        b_last = bt[..., -1:, :]
        state = jnp.exp(b_last[..., 0, :])[..., None] * state + jnp.einsum('bhsk,bhsv->bhkv', kt * jnp.exp(b_last - bt), vt)
        return state, o

    state0 = jnp.zeros((B_, H_, K_, K_), jnp.float32)
    _, o = lax.scan(step, state0, (qc, kc, vc, bc))
    return o.transpose(1, 0, 3, 2, 4).reshape(B_, S_, H_, K_)


def setup_inputs(seed: int = 0) -> dict:
    key = jax.random.key(seed)
    ks = jax.random.split(key, 20)
    f32 = jnp.float32

    def w(k, shape, fan_in):
        return jax.random.normal(k, shape, f32) * fan_in ** -0.5

    def gain(k, shape):
        return 1.0 + 0.02 * jax.random.normal(k, shape, f32)

    return {
        "x": jax.random.normal(ks[0], (BATCH, SEQ, D_MODEL), f32),
        "mem": jax.random.normal(ks[1], (BATCH, N_MEM, D_MODEL), f32),
        "norm1_g": gain(ks[2], (DEPTH, D_MODEL)),
        "w_in": w(ks[3], (DEPTH, D_MODEL, IN_W), D_MODEL),
        "attn_qn_g": gain(ks[4], (DEPTH, ATT_HEAD_DIM)),
        "attn_kn_g": gain(ks[5], (DEPTH, ATT_HEAD_DIM)),
        "hg_lb": 0.1 * jax.random.normal(ks[6], (DEPTH, HG_W), f32),
        "hg_onorm_g": gain(ks[7], (DEPTH, HG_W)),
        "w_out": w(ks[8], (DEPTH, MIX_W, D_MODEL), MIX_W),
        "norm2_g": gain(ks[9], (DEPTH, D_MODEL)),
        "mem_norm_g": gain(ks[10], (DEPTH, D_MODEL)),
        "w_mq": w(ks[11], (DEPTH, D_MODEL, D_MODEL), D_MODEL),
        "w_mkv": w(ks[12], (DEPTH, D_MODEL, 2 * D_MODEL), D_MODEL),
        "mq_norm_g": gain(ks[13], (DEPTH, MEM_HEAD_DIM)),
        "mk_norm_g": gain(ks[14], (DEPTH, MEM_HEAD_DIM)),
        "w_mo": w(ks[15], (DEPTH, D_MODEL, D_MODEL), D_MODEL),
        "norm3_g": gain(ks[16], (DEPTH, D_MODEL)),
        "w_ff1": w(ks[17], (DEPTH, D_MODEL, D_FF), D_MODEL),
        "w_ff2": w(ks[18], (DEPTH, D_FF, D_MODEL), D_FF),
    }


def reference(x, mem, norm1_g, w_in, attn_qn_g, attn_kn_g, hg_lb, hg_onorm_g, w_out,
              norm2_g, mem_norm_g, w_mq, w_mkv, mq_norm_g, mk_norm_g, w_mo,
              norm3_g, w_ff1, w_ff2):
    B_, S_, _ = x.shape
    f32 = jnp.float32
    slopes = jnp.exp2(-8.0 / ATT_HEADS * jnp.arange(1, ATT_HEADS + 1, dtype=f32))
    p_lb = jax.nn.softmax(hg_lb.astype(f32), axis=0)
    lower_bounds = jnp.cumsum(p_lb, axis=0) - p_lb[0:1]
    cuts = np.cumsum([ATT_W, ATT_W, ATT_W, HG_W, HG_W, HG_W]).tolist()

    for l in range(DEPTH):
        h = rms_norm(x, norm1_g[l])
        z = h @ w_in[l]
        qa, ka, va, qh, fh, ih, gh = jnp.split(z, cuts, axis=-1)
        qa = rms_norm(qa.reshape(B_, S_, ATT_HEADS, ATT_HEAD_DIM), attn_qn_g[l]).astype(f32)
        ka = rms_norm(ka.reshape(B_, S_, ATT_HEADS, ATT_HEAD_DIM), attn_kn_g[l]).astype(f32)
        va = va.reshape(B_, S_, ATT_HEADS, ATT_HEAD_DIM).astype(f32)
        outs, lses = [], []
        for window, dilation in DILATED_PATTERNS:
            o_b, lse_b = dilated_branch(qa, ka, va, window, dilation, slopes)
            outs.append(o_b)
            lses.append(lse_b)
        wts = jax.nn.softmax(jnp.stack(lses, axis=0), axis=0)
        ya = jnp.sum(wts[..., None] * jnp.stack(outs, axis=0), axis=0).reshape(B_, S_, ATT_W)

        hs = (B_, S_, HG_HEADS, HG_HEAD_DIM)
        oh = hgrn2_recurrence(qh.reshape(hs), fh.reshape(hs), ih.reshape(hs),
                              lower_bounds[l].reshape(HG_HEADS, HG_HEAD_DIM))
        oh = rms_norm(oh, hg_onorm_g[l].reshape(HG_HEADS, HG_HEAD_DIM)).reshape(B_, S_, HG_W)
        yh = oh * jax.nn.silu(gh.astype(f32))

        y = jnp.concatenate([ya, yh], axis=-1).astype(x.dtype) @ w_out[l]
        x = x + y

        h = rms_norm(x, norm2_g[l])
        mn = rms_norm(mem, mem_norm_g[l])
        qm = rms_norm((h @ w_mq[l]).reshape(B_, S_, MEM_HEADS, MEM_HEAD_DIM), mq_norm_g[l]).astype(f32)
        km, vm = jnp.split(mn @ w_mkv[l], 2, axis=-1)
        km = rms_norm(km.reshape(B_, -1, MEM_HEADS, MEM_HEAD_DIM), mk_norm_g[l]).astype(f32)
        vm = vm.reshape(B_, -1, MEM_HEADS, MEM_HEAD_DIM).astype(f32)
        sm = jnp.einsum('bshe,bmhe->bhsm', qm, km) * (1.0 / math.sqrt(MEM_HEAD_DIM))
        pm = jax.nn.softmax(sm, axis=-1)
        om = jnp.einsum('bhsm,bmhe->bshe', pm, vm).reshape(B_, S_, D_MODEL)
        x = x + om.astype(x.dtype) @ w_mo[l]

        h = rms_norm(x, norm3_g[l])
        u = jnp.square(jax.nn.relu(h @ w_ff1[l]))
        x = x + u @ w_ff2[l]
    return x
```

```python
import functools
import math

import jax
import jax.numpy as jnp
import numpy as np
from jax import lax
from jax.experimental import pallas as pl
from jax.experimental.pallas import tpu as pltpu

F32 = jnp.float32
BF16 = jnp.bfloat16

EPS = 1e-6
NEG = -1e30
F_MIN = 1e-12

ATT_HEADS = 8
ATT_HEAD_DIM = 64
ATT_W = ATT_HEADS * ATT_HEAD_DIM
DILATED_PATTERNS = ((128, 1), (512, 4), (2048, 16))
BLK = 128
HG_HEADS = 4
HG_HEAD_DIM = 128
HG_W = HG_HEADS * HG_HEAD_DIM
MEM_HEADS = 4

LANES = 128
SUBLANES = 8
V7X_VMEM_BYTES = 64 * 1024 * 1024
VMEM_LIMIT = V7X_VMEM_BYTES - 8 * 1024 * 1024

HG_CHUNK = 128
HG_LEVELS = tuple(1 << i for i in range(int(math.log2(HG_CHUNK))))


def _rms(x, g):
    return x * lax.rsqrt(jnp.mean(x * x, axis=-1, keepdims=True) + EPS) * g


def _dot(a, b):
    return jnp.dot(a, b, preferred_element_type=F32)


def _dot_nt(a, b):
    return lax.dot_general(a, b, (((1,), (1,)), ((), ())), preferred_element_type=F32)


def _dot_tn(a, b):
    return lax.dot_general(a, b, (((0,), (0,)), ((), ())), preferred_element_type=F32)


def _params(*semantics):
    return pltpu.CompilerParams(dimension_semantics=semantics, vmem_limit_bytes=VMEM_LIMIT)


def _resident(shape):
    nd = len(shape)
    return pl.BlockSpec(shape, lambda *_: (0,) * nd, pipeline_mode=pl.Buffered(1))


def _in_proj_kernel(x_ref, g_ref, w_ref, z_ref):
    h = _rms(x_ref[...], g_ref[...]).astype(BF16)
    n_out = z_ref.shape[-1]
    step = 512
    for c in range(0, n_out, step):
        z_ref[:, c:c + step] = _dot(h, w_ref[:, c:c + step])


def _in_proj(x2d, g, w, tm):
    n, d = x2d.shape
    n_out = w.shape[1]
    return pl.pallas_call(
        _in_proj_kernel,
        out_shape=jax.ShapeDtypeStruct((n, n_out), F32),
        grid=(n // tm,),
        in_specs=[
            pl.BlockSpec((tm, d), lambda i: (i, 0)),
            _resident((1, d)),
            _resident((d, n_out)),
        ],
        out_specs=pl.BlockSpec((tm, n_out), lambda i: (i, 0)),
        compiler_params=_params("parallel"),
        name="in_proj",
    )(x2d, g, w)


def _attn_bias_tables():
    slopes = 2.0 ** (-8.0 / ATT_HEADS * np.arange(1, ATT_HEADS + 1))
    qi = np.arange(BLK)[:, None]
    kj = np.arange(2 * BLK)[None, :]
    step = qi + BLK - kj
    tables = []
    for window, dilation in DILATED_PATTERNS:
        steps = window // dilation
        assert steps == BLK
        valid = (step >= 0) & (step <= steps)
        bias = -slopes[:, None, None] * (step * dilation).astype(np.float64)[None]
        tables.append(np.where(valid[None], bias, NEG))
    return jnp.asarray(np.stack(tables), F32)


def _attn_kernel(q_ref, k_ref, v_ref, qg_ref, kg_ref, bias_ref, o_ref, qn, kn, ob, lb):
    seq = q_ref.shape[0]
    lane = lax.broadcasted_iota(jnp.int32, (1, LANES), 1)
    first = lane < ATT_HEAD_DIM

    def head_norm(x, g):
        x2 = x * x
        sa = jnp.sum(jnp.where(first, x2, 0.0), axis=-1, keepdims=True)
        sb = jnp.sum(jnp.where(first, 0.0, x2), axis=-1, keepdims=True)
        ms = jnp.where(first, sa, sb) * (1.0 / ATT_HEAD_DIM)
        return x * lax.rsqrt(ms + EPS) * g

    rows = 256
    scale = 1.0 / math.sqrt(ATT_HEAD_DIM)

    def norm_body(i, carry):
        r = pl.ds(pl.multiple_of(i * rows, rows), rows)
        qn[r, :] = head_norm(q_ref[r, :], qg_ref[...]) * scale
        kn[r, :] = head_norm(k_ref[r, :], kg_ref[...])
        return carry

    lax.fori_loop(0, seq // rows, norm_body, 0)

    def block(qb, kb, vb, pi, with_prev):
        kbb = kb.astype(BF16)
        vbb = vb.astype(BF16)
        outs, lses = [], []
        for h in range(2):
            qh = jnp.where(first if h == 0 else jnp.logical_not(first), qb, 0.0).astype(BF16)
            bias = bias_ref[pi, h] if with_prev else bias_ref[pi, h, :, BLK:]
            s = _dot_nt(qh, kbb) + bias
            m = jnp.max(s, axis=-1, keepdims=True)
            p = jnp.exp(s - m)
            l = jnp.sum(p, axis=-1, keepdims=True)
            o = _dot(p.astype(BF16), vbb)
            outs.append(o / l)
            lses.append(m + jnp.log(l))
        return jnp.where(first, outs[0], outs[1]), jnp.where(first, lses[0], lses[1])

    def run_pattern(pi, dilation):
        length = seq // dilation
        n_blocks = length // BLK

        def rows_of(r, blk, n):
            start = r + blk * (BLK * dilation)
            if dilation == 1:
                return pl.ds(pl.multiple_of(start, BLK), n)
            return pl.ds(start, n, stride=dilation)

        def residue_body(r, carry):
            sl = rows_of(r, 0, BLK)
            o, lse = block(qn[sl, :], kn[sl, :], v_ref[sl, :], pi, False)
            ob[pi, sl, :] = o
            lb[pi, sl, :] = lse

            def blk_body(n, c2):
                sq = rows_of(r, n, BLK)
                sk = rows_of(r, n - 1, 2 * BLK)
                o2, lse2 = block(qn[sq, :], kn[sk, :], v_ref[sk, :], pi, True)
                ob[pi, sq, :] = o2
                lb[pi, sq, :] = lse2
                return c2

            if n_blocks > 1:
                lax.fori_loop(1, n_blocks, blk_body, 0)
            return carry

        if dilation == 1:
            residue_body(0, 0)
        else:
            lax.fori_loop(0, dilation, residue_body, 0)

    for pi, (_, dilation) in enumerate(DILATED_PATTERNS):
        run_pattern(pi, dilation)

    def combine_body(i, carry):
        r = pl.ds(pl.multiple_of(i * rows, rows), rows)
        l0, l1, l2 = lb[0, r, :], lb[1, r, :], lb[2, r, :]
        m = jnp.maximum(jnp.maximum(l0, l1), l2)
        w0, w1, w2 = jnp.exp(l0 - m), jnp.exp(l1 - m), jnp.exp(l2 - m)
        num = w0 * ob[0, r, :] + w1 * ob[1, r, :] + w2 * ob[2, r, :]
        o_ref[r, :] = (num / (w0 + w1 + w2)).astype(o_ref.dtype)
        return carry

    lax.fori_loop(0, seq // rows, combine_body, 0)


def _dilated_attention(z, qg, kg, bias):
    b, s, _ = z.shape
    pairs = ATT_W // LANES
    assert s // DILATED_PATTERNS[-1][1] == BLK
    col = lambda off: pl.BlockSpec((None, s, LANES), lambda bi, hp: (bi, 0, off + hp))
    qg2 = jnp.tile(qg, 2).reshape(1, LANES)
    kg2 = jnp.tile(kg, 2).reshape(1, LANES)
    return pl.pallas_call(
        _attn_kernel,
        out_shape=jax.ShapeDtypeStruct((b, s, ATT_W), BF16),
        grid=(b, pairs),
        in_specs=[
            col(0), col(pairs), col(2 * pairs),
            _resident((1, LANES)), _resident((1, LANES)),
            pl.BlockSpec((len(DILATED_PATTERNS), 2, BLK, 2 * BLK), lambda bi, hp: (0, hp, 0, 0)),
        ],
        out_specs=pl.BlockSpec((None, s, LANES), lambda bi, hp: (bi, 0, hp)),
        scratch_shapes=[
            pltpu.VMEM((s, LANES), F32), pltpu.VMEM((s, LANES), F32),
            pltpu.VMEM((len(DILATED_PATTERNS), s, LANES), F32),
            pltpu.VMEM((len(DILATED_PATTERNS), s, LANES), F32),
        ],
        compiler_params=_params("parallel", "parallel"),
        name="dilated_attn",
    )(z, z, z, qg2, kg2, bias)


def _hgrn_tables():
    t = np.arange(HG_CHUNK)[:, None]
    s = np.arange(HG_CHUNK)[None, :]
    ltri = (s <= t).astype(np.float32)
    masks = []
    for m in HG_LEVELS:
        same_pair = (t // (2 * m)) == (s // (2 * m))
        masks.append((same_pair & (t % (2 * m) >= m) & (s % (2 * m) < m)).astype(np.float32))
    return jnp.asarray(ltri, BF16), jnp.asarray(np.stack(masks), F32)


def _pair_reference(b, m):
    c = HG_CHUNK
    if m >= SUBLANES:
        pieces = []
        for p in range(c // (2 * m)):
            i0 = 2 * m * p + m
            pieces.append(jnp.broadcast_to(b[i0:i0 + 1, :], (2 * m, LANES)))
        return jnp.concatenate(pieces, axis=0)
    groups = c // SUBLANES
    b3 = b.reshape(groups, SUBLANES, LANES)
    sub = lax.broadcasted_iota(jnp.int32, (groups, SUBLANES, LANES), 1)
    ref = None
    for p in range(SUBLANES // (2 * m)):
        i0 = 2 * m * p + m
        piece = jnp.broadcast_to(b3[:, i0:i0 + 1, :], (groups, SUBLANES, LANES))
        ref = piece if ref is None else jnp.where(sub >= 2 * m * p, piece, ref)
    return ref.reshape(c, LANES)


def _hgrn_chunk(qraw, fl, iv, lbound, ltri, lmask_ref, state_t, row):
    q = qraw * (1.0 / (1.0 + jnp.exp(-qraw)))
    sg = 1.0 / (1.0 + jnp.exp(-fl))
    f = lbound + (1.0 - lbound) * sg
    lf = jnp.log(jnp.maximum(f, F_MIN))
    kk = (1.0 - lbound) * (1.0 - sg)
    lf_hi = lf.astype(BF16)
    lf_lo = (lf - lf_hi.astype(F32)).astype(BF16)
    b = _dot(ltri, lf_hi) + _dot(ltri, lf_lo)
    vb = iv.astype(BF16)

    a = jnp.zeros((HG_CHUNK, HG_CHUNK), F32)
    for li, m in enumerate(HG_LEVELS):
        e = jnp.exp(-jnp.abs(b - _pair_reference(b, m)))
        earlier = (row & m) == 0
        xs = (jnp.where(earlier, kk, q) * e).astype(BF16)
        a = a + lmask_ref[li] * _dot_nt(xs, xs)
    diag = jnp.sum(q * kk, axis=-1, keepdims=True)
    o = _dot(a.astype(BF16), vb) + diag * iv

    o = o + _dot_nt((q * jnp.exp(b)).astype(BF16), state_t.astype(BF16))
    b_last = b[HG_CHUNK - 1:HG_CHUNK, :]
    k_end = (kk * jnp.exp(b_last - b)).astype(BF16)
    state_t = state_t * jnp.exp(b_last) + _dot_tn(vb, k_end)
    return o, state_t


def _hgrn_kernel(q_ref, f_ref, i_ref, g_ref, lb_ref, og_ref, ltri_ref, lmask_ref, o_ref, st_ref):
    @pl.when(pl.program_id(2) == 0)
    def _():
        st_ref[...] = jnp.zeros_like(st_ref)

    n_chunks = q_ref.shape[0] // HG_CHUNK
    row = lax.broadcasted_iota(jnp.int32, (HG_CHUNK, LANES), 0)
    ltri = ltri_ref[...]

    def chunk_body(c, carry):
        r = pl.ds(pl.multiple_of(c * HG_CHUNK, HG_CHUNK), HG_CHUNK)
        for h in range(2):
            cs = slice(h * LANES, (h + 1) * LANES)
            o, st = _hgrn_chunk(q_ref[r, cs], f_ref[r, cs], i_ref[r, cs], lb_ref[:, cs],
                                ltri, lmask_ref, st_ref[h], row)
            st_ref[h] = st
            g = g_ref[r, cs]
            y = _rms(o, og_ref[:, cs]) * (g * (1.0 / (1.0 + jnp.exp(-g))))
            o_ref[r, cs] = y.astype(o_ref.dtype)
        return carry

    lax.fori_loop(0, n_chunks, chunk_body, 0)


def _hgrn(z, lbound, onorm_g, ltri, lmask, ts):
    b, s, _ = z.shape
    width = 2 * LANES
    pairs = HG_W // width
    base = 3 * ATT_W // width
    col = lambda k: pl.BlockSpec((None, ts, width), lambda bi, hp, si: (bi, si, base + k * pairs + hp))
    par = pl.BlockSpec((1, width), lambda bi, hp, si: (0, hp))
    return pl.pallas_call(
        _hgrn_kernel,
        out_shape=jax.ShapeDtypeStruct((b, s, HG_W), BF16),
        grid=(b, pairs, s // ts),
        in_specs=[col(0), col(1), col(2), col(3), par, par,
                  _resident(ltri.shape), _resident(lmask.shape)],
        out_specs=pl.BlockSpec((None, ts, width), lambda bi, hp, si: (bi, si, hp)),
        scratch_shapes=[pltpu.VMEM((2, HG_HEAD_DIM, HG_HEAD_DIM), F32)],
        compiler_params=_params("parallel", "parallel", "arbitrary"),
        name="hgrn2",
    )(z, z, z, z, lbound, onorm_g, ltri, lmask)


def _mem_kv_kernel(mem_ref, g_ref, w_ref, kg_ref, km_ref, vm_ref):
    d = mem_ref.shape[-1]
    hd = d // MEM_HEADS
    mn = _rms(mem_ref[...], g_ref[...]).astype(BF16)
    for h in range(MEM_HEADS):
        k = _dot(mn, w_ref[:, h * hd:(h + 1) * hd])
        km_ref[:, h * hd:(h + 1) * hd] = _rms(k, kg_ref[...]).astype(BF16)
    vm_ref[...] = _dot(mn, w_ref[:, d:]).astype(BF16)


def _mem_kv(mem, g, w, kg):
    b, n_mem, d = mem.shape
    blk = pl.BlockSpec((None, n_mem, d), lambda bi: (bi, 0, 0))
    return pl.pallas_call(
        _mem_kv_kernel,
        out_shape=(jax.ShapeDtypeStruct((b, n_mem, d), BF16),) * 2,
        grid=(b,),
        in_specs=[blk, _resident((1, d)), _resident(w.shape), _resident((1, d // MEM_HEADS))],
        out_specs=(blk, blk),
        compiler_params=_params("parallel"),
        name="mem_kv",
    )(mem, g, w, kg)


def _mix_mem_kernel(x_ref, ya_ref, yh_ref, wo_ref, g_ref, wq_ref, qg_ref, km_ref, vm_ref, wmo_ref,
                    o_ref):
    d = x_ref.shape[-1]
    hd = d // MEM_HEADS
    y = _dot(ya_ref[...], wo_ref[:ATT_W, :]) + _dot(yh_ref[...], wo_ref[ATT_W:, :])
    x1 = x_ref[...] + y
    h = _rms(x1, g_ref[...]).astype(BF16)
    scale = 1.0 / math.sqrt(hd)
    heads = []
    for hi in range(MEM_HEADS):
        cs = slice(hi * hd, (hi + 1) * hd)
        qh = (_rms(_dot(h, wq_ref[:, cs]), qg_ref[...]) * scale).astype(BF16)
        s = _dot_nt(qh, km_ref[:, cs])
        m = jnp.max(s, axis=-1, keepdims=True)
        p = jnp.exp(s - m)
        l = jnp.sum(p, axis=-1, keepdims=True)
        heads.append((_dot(p.astype(BF16), vm_ref[:, cs]) / l).astype(BF16))
    om = jnp.concatenate(heads, axis=-1)
    o_ref[...] = x1 + _dot(om, wmo_ref[...])


def _mix_mem(x2d, ya, yh, wo, g, wq, qg, km, vm, wmo, tm, seq):
    n, d = x2d.shape
    n_mem = km.shape[1]
    tiles_per_seq = seq // tm
    tok = lambda w: pl.BlockSpec((tm, w), lambda i: (i, 0))
    mem_blk = pl.BlockSpec((None, n_mem, d), lambda i: (i // tiles_per_seq, 0, 0))
    return pl.pallas_call(
        _mix_mem_kernel,
        out_shape=jax.ShapeDtypeStruct((n, d), F32),
        grid=(n // tm,),
        in_specs=[tok(d), tok(ATT_W), tok(HG_W), _resident(wo.shape), _resident((1, d)),
                  _resident(wq.shape), _resident((1, d // MEM_HEADS)), mem_blk, mem_blk,
                  _resident(wmo.shape)],
        out_specs=tok(d),
        compiler_params=_params("parallel"),
        name="mix_mem",
    )(x2d, ya, yh, wo, g, wq, qg, km, vm, wmo)


def _ffn_kernel(x_ref, g_ref, w1_ref, w2_ref, o_ref):
    x = x_ref[...]
    h = _rms(x, g_ref[...]).astype(BF16)
    d_ff = w1_ref.shape[1]
    step = 1024
    acc = x
    for c in range(0, d_ff, step):
        u = jnp.maximum(_dot(h, w1_ref[:, c:c + step]), 0.0)
        acc = acc + _dot((u * u).astype(BF16), w2_ref[c:c + step, :])
    o_ref[...] = acc


def _ffn(x2d, g, w1, w2, tm):
    n, d = x2d.shape
    tok = pl.BlockSpec((tm, d), lambda i: (i, 0))
    return pl.pallas_call(
        _ffn_kernel,
        out_shape=jax.ShapeDtypeStruct((n, d), F32),
        grid=(n // tm,),
        in_specs=[tok, _resident((1, d)), _resident(w1.shape), _resident(w2.shape)],
        out_specs=tok,
        compiler_params=_params("parallel"),
        name="ffn",
    )(x2d, g, w1, w2)


def kernel(x, mem, norm1_g, w_in, attn_qn_g, attn_kn_g, hg_lb, hg_onorm_g, w_out, norm2_g,
           mem_norm_g, w_mq, w_mkv, mq_norm_g, mk_norm_g, w_mo, norm3_g, w_ff1, w_ff2):
    bsz, seq, d = x.shape
    depth = w_in.shape[0]
    tm = 512
    assert seq % tm == 0 and seq % HG_CHUNK == 0

    p_lb = jax.nn.softmax(hg_lb.astype(F32), axis=0)
    lower_bounds = jnp.cumsum(p_lb, axis=0) - p_lb[0:1]
    bias = _attn_bias_tables()
    ltri, lmask = _hgrn_tables()
    row = lambda v: v.reshape(1, -1).astype(F32)

    x2d = x.reshape(bsz * seq, d)
    for l in range(depth):
        z = _in_proj(x2d, row(norm1_g[l]), w_in[l].astype(BF16), tm).reshape(bsz, seq, -1)
        ya = _dilated_attention(z, attn_qn_g[l].astype(F32), attn_kn_g[l].astype(F32), bias)
        yh = _hgrn(z, row(lower_bounds[l]), row(hg_onorm_g[l]), ltri, lmask, tm)
        km, vm = _mem_kv(mem, row(mem_norm_g[l]), w_mkv[l].astype(BF16), row(mk_norm_g[l]))
        x2d = _mix_mem(x2d, ya.reshape(bsz * seq, ATT_W), yh.reshape(bsz * seq, HG_W),
                       w_out[l].astype(BF16), row(norm2_g[l]), w_mq[l].astype(BF16),
                       row(mq_norm_g[l]), km, vm, w_mo[l].astype(BF16), tm, seq)
        x2d = _ffn(x2d, row(norm3_g[l]), w_ff1[l].astype(BF16), w_ff2[l].astype(BF16), tm)
    return x2d.reshape(bsz, seq, d)
```

```python
import math

import jax
import jax.numpy as jnp
import numpy as np
from jax import lax
from jax.experimental import pallas as pl
from jax.experimental.pallas import tpu as pltpu

F32 = jnp.float32
BF16 = jnp.bfloat16

EPS = 1e-6
NEG = -1e30
F_MIN = 1e-12
LOG2E = math.log2(math.e)

ATT_HEADS = 8
ATT_HEAD_DIM = 64
ATT_W = ATT_HEADS * ATT_HEAD_DIM
DILATED_PATTERNS = ((128, 1), (512, 4), (2048, 16))
BLK = 128
HG_HEADS = 4
HG_HEAD_DIM = 128
HG_W = HG_HEADS * HG_HEAD_DIM
MEM_HEADS = 4

LANES = 128
SUBLANES = 8
V7X_VMEM_BYTES = 64 * 1024 * 1024
VMEM_LIMIT = V7X_VMEM_BYTES - 8 * 1024 * 1024

ATTN_GROUP = 8
HG_CHUNK = 128
HG_LEVELS = tuple(1 << i for i in range(int(math.log2(HG_CHUNK))))


def _rms(x, g):
    return x * lax.rsqrt(jnp.mean(x * x, axis=-1, keepdims=True) + EPS) * g


def _sigmoid(x):
    return 1.0 / (1.0 + jnp.exp(-x))


def _dot(a, b):
    return jnp.dot(a, b, preferred_element_type=F32)


def _dot_nt(a, b):
    return lax.dot_general(a, b, (((1,), (1,)), ((), ())), preferred_element_type=F32)


def _dot_tn(a, b):
    return lax.dot_general(a, b, (((0,), (0,)), ((), ())), preferred_element_type=F32)


def _params(*semantics):
    return pltpu.CompilerParams(dimension_semantics=semantics, vmem_limit_bytes=VMEM_LIMIT)


def _resident(shape):
    nd = len(shape)
    return pl.BlockSpec(shape, lambda *_: (0,) * nd, pipeline_mode=pl.Buffered(1))


def _in_proj_kernel(x_ref, g_ref, w_ref, z_ref):
    h = _rms(x_ref[...], g_ref[...]).astype(BF16)
    n_out = z_ref.shape[-1]
    step = 512
    for c in range(0, n_out, step):
        z_ref[:, c:c + step] = _dot(h, w_ref[:, c:c + step])


def _in_proj(x2d, g, w, tm):
    n, d = x2d.shape
    n_out = w.shape[1]
    return pl.pallas_call(
        _in_proj_kernel,
        out_shape=jax.ShapeDtypeStruct((n, n_out), F32),
        grid=(n // tm,),
        in_specs=[
            pl.BlockSpec((tm, d), lambda i: (i, 0)),
            _resident((1, d)),
            _resident((d, n_out)),
        ],
        out_specs=pl.BlockSpec((tm, n_out), lambda i: (i, 0)),
        compiler_params=_params("parallel"),
        name="in_proj",
    )(x2d, g, w)


def _attn_bias_tables():
    slopes = 2.0 ** (-8.0 / ATT_HEADS * np.arange(1, ATT_HEADS + 1))
    qi = np.arange(BLK)[:, None]
    kj = np.arange(2 * BLK)[None, :]
    step = qi + BLK - kj
    tables = []
    for window, dilation in DILATED_PATTERNS:
        steps = window // dilation
        assert steps == BLK
        valid = (step >= 0) & (step <= steps)
        bias = -slopes[:, None, None] * (step * dilation).astype(np.float64)[None] * LOG2E
        t = np.where(valid[None], bias, NEG)
        tables.append(t.reshape(ATT_HEADS // 2, 2 * BLK, 2 * BLK))
    return jnp.asarray(np.stack(tables), F32)


def _attn_kernel(q_ref, k_ref, v_ref, qg_ref, kg_ref, seg_ref, bias_ref, o_ref,
                 qn, kn, q4, k4, v4, qs, ks, vs, acc4, m4, l4, acc1, m1, l1):
    seq = q_ref.shape[0]
    n_pat = len(DILATED_PATTERNS)
    assert [d for _, d in DILATED_PATTERNS] == [1, 4, 16] and seq == 16 * BLK
    lane = lax.broadcasted_iota(jnp.int32, (1, LANES), 1)
    first = lane < ATT_HEAD_DIM

    def head_norm(x, g):
        x2 = x * x
        hi = x2.astype(BF16)
        lo = (x2 - hi.astype(F32)).astype(BF16)
        ms = (_dot(hi, seg_ref[...]) + _dot(lo, seg_ref[...])) * (1.0 / ATT_HEAD_DIM)
        return x * lax.rsqrt(ms + EPS) * g

    def put_operands(pi, blk, q, k, v):
        r = pl.ds(pl.multiple_of(blk * BLK, BLK), BLK)
        qs[pi, blk, :BLK, :] = jnp.where(first, q, 0.0).astype(BF16)
        qs[pi, blk, BLK:, :] = jnp.where(first, 0.0, q).astype(BF16)
        ks[pi, r, :] = k.astype(BF16)
        vs[pi, r, :LANES] = v.astype(BF16)
        vs[pi, r, LANES:] = jnp.ones((BLK, LANES), BF16)

    scale = LOG2E / math.sqrt(ATT_HEAD_DIM)

    def norm_body(i, carry):
        r = pl.ds(pl.multiple_of(i * BLK, BLK), BLK)
        q = head_norm(q_ref[r, :], qg_ref[...]) * scale
        k = head_norm(k_ref[r, :], kg_ref[...])
        qn[r, :] = q
        kn[r, :] = k
        put_operands(0, i, q, k, v_ref[r, :])
        return carry

    lax.fori_loop(0, seq // BLK, norm_body, 0, unroll=4)

    quarter = seq // 4

    def sort4_body(i, carry):
        res = i // 4
        src = pl.ds(res + (i % 4) * (4 * BLK), BLK, stride=4)
        dst = pl.ds(pl.multiple_of(i * BLK, BLK), BLK)
        q, k, v = qn[src, :], kn[src, :], v_ref[src, :]
        q4[dst, :] = q
        k4[dst, :] = k
        v4[dst, :] = v
        put_operands(1, i, q, k, v)
        return carry

    lax.fori_loop(0, seq // BLK, sort4_body, 0)

    def sort16_body(i, carry):
        src = pl.ds((i % 4) * quarter + i // 4, BLK, stride=4)
        put_operands(2, i, q4[src, :], k4[src, :], v4[src, :])
        return carry

    lax.fori_loop(0, seq // BLK, sort16_body, 0)

    def key_rows(blk, with_prev):
        if with_prev:
            return pl.ds(pl.multiple_of((blk - 1) * BLK, BLK), 2 * BLK)
        return pl.ds(pl.multiple_of(blk * BLK, BLK), BLK)

    def scores(pi, blk, with_prev):
        bias = bias_ref[pi] if with_prev else bias_ref[pi, :, BLK:]
        return _dot_nt(qs[pi, blk], ks[pi, key_rows(blk, with_prev), :]) + bias

    def block(pi, blk, with_prev, s):
        keys = key_rows(blk, with_prev)
        m = jnp.max(s, axis=-1, keepdims=True)
        p = jnp.exp2(s - m).astype(BF16)
        r = _dot(p, vs[pi, keys, :])
        acc = jnp.where(first, r[:BLK, :LANES], r[BLK:, :LANES])
        l = jnp.where(first, r[:BLK, LANES:], r[BLK:, LANES:])
        mm = jnp.where(first, m[:BLK], m[BLK:])
        return acc, mm, l

    def merge(a, b):
        m = jnp.maximum(a[1], b[1])
        wa = jnp.exp2(a[1] - m)
        wb = jnp.exp2(b[1] - m)
        return wa * a[0] + wb * b[0], m, wa * a[2] + wb * b[2]

    n_groups = seq // BLK // ATTN_GROUP
    assert ATTN_GROUP % 4 == 0

    def grouped(pi, has_prev, consume, groups=(0, n_groups)):
        def group_body(g, carry):
            s_next = scores(pi, g * ATTN_GROUP, has_prev(0))
            for j in range(ATTN_GROUP):
                s_cur = s_next
                if j + 1 < ATTN_GROUP:
                    s_next = scores(pi, g * ATTN_GROUP + j + 1, has_prev(j + 1))
                consume(g, j, block(pi, g * ATTN_GROUP + j, has_prev(j), s_cur))
            return carry

        if groups[1] - groups[0] == 1:
            group_body(groups[0], 0)
        elif groups[1] > groups[0]:
            lax.fori_loop(groups[0], groups[1], group_body, 0)

    def p16(g, j, res):
        a = g * (ATTN_GROUP // 4) + j // 4
        dst = pl.ds((j % 4) * quarter + a, BLK, stride=4)
        acc4[dst, :], m4[dst, :], l4[dst, :] = res

    grouped(2, lambda j: False, p16)

    def p4(g, j, res):
        blk = g * ATTN_GROUP + j
        r4 = g * (ATTN_GROUP // 4) + j // 4
        src = pl.ds(pl.multiple_of(blk * BLK, BLK), BLK)
        dst = pl.ds(r4 + (j % 4) * (4 * BLK), BLK, stride=4)
        acc1[dst, :], m1[dst, :], l1[dst, :] = merge(res, (acc4[src, :], m4[src, :], l4[src, :]))

    grouped(1, lambda j: j % 4 > 0, p4)

    def p1(g, j, res):
        src = pl.ds(pl.multiple_of((g * ATTN_GROUP + j) * BLK, BLK), BLK)
        acc, _, l = merge(res, (acc1[src, :], m1[src, :], l1[src, :]))
        o_ref[src, :] = (acc / l).astype(o_ref.dtype)

    grouped(0, lambda j: j > 0, p1, groups=(0, 1))
    grouped(0, lambda j: True, p1, groups=(1, n_groups))


def _dilated_attention(z, qg, kg, bias):
    b, s, _ = z.shape
    pairs = ATT_W // LANES
    n_pat = len(DILATED_PATTERNS)
    col = lambda off: pl.BlockSpec((None, s, LANES), lambda bi, hp: (bi, 0, off + hp))
    qg2 = jnp.tile(qg, 2).reshape(1, LANES)
    kg2 = jnp.tile(kg, 2).reshape(1, LANES)
    head_of_lane = np.arange(LANES) // ATT_HEAD_DIM
    seg = jnp.asarray(head_of_lane[:, None] == head_of_lane[None, :], BF16)
    seq_f32 = pltpu.VMEM((s, LANES), F32)
    return pl.pallas_call(
        _attn_kernel,
        out_shape=jax.ShapeDtypeStruct((b, s, ATT_W), BF16),
        grid=(b, pairs),
        in_specs=[
            col(0), col(pairs), col(2 * pairs),
            _resident((1, LANES)), _resident((1, LANES)), _resident((LANES, LANES)),
            pl.BlockSpec((n_pat, None, 2 * BLK, 2 * BLK), lambda bi, hp: (0, hp, 0, 0)),
        ],
        out_specs=pl.BlockSpec((None, s, LANES), lambda bi, hp: (bi, 0, hp)),
        scratch_shapes=[
            seq_f32, seq_f32, seq_f32, seq_f32, seq_f32,
            pltpu.VMEM((n_pat, s // BLK, 2 * BLK, LANES), BF16),
            pltpu.VMEM((n_pat, s, LANES), BF16),
            pltpu.VMEM((n_pat, s, 2 * LANES), BF16),
            seq_f32, seq_f32, seq_f32, seq_f32, seq_f32, seq_f32,
        ],
        compiler_params=_params("parallel", "parallel"),
        name="dilated_attn",
    )(z, z, z, qg2, kg2, seg, bias)


def _hgrn_tables():
    t = np.arange(HG_CHUNK)[:, None]
    s = np.arange(HG_CHUNK)[None, :]
    ltri = (s <= t).astype(np.float32)
    masks, signs = [], []
    for m in HG_LEVELS:
        same_pair = (t // (2 * m)) == (s // (2 * m))
        masks.append((same_pair & (t % (2 * m) >= m) & (s % (2 * m) < m)).astype(np.float32))
        signs.append(np.broadcast_to(np.where(t % (2 * m) < m, 1.0, -1.0), (HG_CHUNK, LANES)))
    return (jnp.asarray(ltri, BF16), jnp.asarray(np.stack(masks), F32),
            jnp.asarray(np.stack(signs), F32))


def _pair_reference(b, m):
    c = HG_CHUNK
    if m >= SUBLANES:
        pieces = []
        for p in range(c // (2 * m)):
            i0 = 2 * m * p + m
            pieces.append(jnp.broadcast_to(b[i0:i0 + 1, :], (2 * m, LANES)))
        return jnp.concatenate(pieces, axis=0)
    groups = c // SUBLANES
    b3 = b.reshape(groups, SUBLANES, LANES)
    sub = lax.broadcasted_iota(jnp.int32, (groups, SUBLANES, LANES), 1)
    ref = None
    for p in range(SUBLANES // (2 * m)):
        i0 = 2 * m * p + m
        piece = jnp.broadcast_to(b3[:, i0:i0 + 1, :], (groups, SUBLANES, LANES))
        ref = piece if ref is None else jnp.where(sub >= 2 * m * p, piece, ref)
    return ref.reshape(c, LANES)


def _hgrn_chunk(qraw, fl, iv, lbound, ltri, lmask_ref, lsign_ref, state_t):
    q = qraw * _sigmoid(qraw)
    sg = _sigmoid(fl)
    f = lbound + (1.0 - lbound) * sg
    lf = jnp.log2(jnp.maximum(f, F_MIN))
    kk = (1.0 - lbound) * (1.0 - sg)
    lf_hi = lf.astype(BF16)
    lf_lo = (lf - lf_hi.astype(F32)).astype(BF16)
    b = _dot(ltri, lf_hi) + _dot(ltri, lf_lo)
    vb = iv.astype(BF16)

    a = jnp.zeros((HG_CHUNK, HG_CHUNK), F32)
    for li, m in enumerate(HG_LEVELS):
        sign = lsign_ref[li]
        e = jnp.exp2((_pair_reference(b, m) - b) * sign)
        xs = (jnp.where(sign > 0.0, kk, q) * e).astype(BF16)
        a = a + lmask_ref[li] * _dot_nt(xs, xs)
    diag = jnp.sum(q * kk, axis=-1, keepdims=True)
    o = _dot(a.astype(BF16), vb) + diag * iv

    o = o + _dot_nt((q * jnp.exp2(b)).astype(BF16), state_t.astype(BF16))
    b_last = b[HG_CHUNK - 1:HG_CHUNK, :]
    k_end = (kk * jnp.exp2(b_last - b)).astype(BF16)
    state_t = state_t * jnp.exp2(b_last) + _dot_tn(vb, k_end)
    return o, state_t


def _hgrn_kernel(q_ref, f_ref, i_ref, g_ref, lb_ref, og_ref, ltri_ref, lmask_ref, lsign_ref,
                 o_ref, st_ref):
    @pl.when(pl.program_id(2) == 0)
    def _():
        st_ref[...] = jnp.zeros_like(st_ref)

    n_chunks = q_ref.shape[0] // HG_CHUNK
    ltri = ltri_ref[...]

    def chunk_body(c, carry):
        r = pl.ds(pl.multiple_of(c * HG_CHUNK, HG_CHUNK), HG_CHUNK)
        for h in range(2):
            cs = slice(h * LANES, (h + 1) * LANES)
            o, st = _hgrn_chunk(q_ref[r, cs], f_ref[r, cs], i_ref[r, cs], lb_ref[:, cs],
                                ltri, lmask_ref, lsign_ref, st_ref[h])
            st_ref[h] = st
            g = g_ref[r, cs]
            y = _rms(o, og_ref[:, cs]) * (g * _sigmoid(g))
            o_ref[r, cs] = y.astype(o_ref.dtype)
        return carry

    lax.fori_loop(0, n_chunks, chunk_body, 0, unroll=2)


def _hgrn(z, lbound, onorm_g, tables, ts):
    b, s, _ = z.shape
    width = 2 * LANES
    pairs = HG_W // width
    base = 3 * ATT_W // width
    col = lambda k: pl.BlockSpec((None, ts, width), lambda bi, hp, si: (bi, si, base + k * pairs + hp))
    par = pl.BlockSpec((1, width), lambda bi, hp, si: (0, hp))
    return pl.pallas_call(
        _hgrn_kernel,
        out_shape=jax.ShapeDtypeStruct((b, s, HG_W), BF16),
        grid=(b, pairs, s // ts),
        in_specs=[col(0), col(1), col(2), col(3), par, par] + [_resident(t.shape) for t in tables],
        out_specs=pl.BlockSpec((None, ts, width), lambda bi, hp, si: (bi, si, hp)),
        scratch_shapes=[pltpu.VMEM((2, HG_HEAD_DIM, HG_HEAD_DIM), F32)],
        compiler_params=_params("parallel", "parallel", "arbitrary"),
        name="hgrn2",
    )(z, z, z, z, lbound, onorm_g, *tables)


def _mem_kv_kernel(mem_ref, g_ref, w_ref, kg_ref, km_ref, vm_ref):
    d = mem_ref.shape[-1]
    hd = d // MEM_HEADS
    mn = _rms(mem_ref[...], g_ref[...]).astype(BF16)
    for h in range(MEM_HEADS):
        k = _dot(mn, w_ref[:, h * hd:(h + 1) * hd])
        km_ref[:, h * hd:(h + 1) * hd] = _rms(k, kg_ref[...]).astype(BF16)
    vm_ref[...] = _dot(mn, w_ref[:, d:]).astype(BF16)


def _mem_kv(mem, g, w, kg):
    b, n_mem, d = mem.shape
    blk = pl.BlockSpec((None, n_mem, d), lambda bi: (bi, 0, 0))
    return pl.pallas_call(
        _mem_kv_kernel,
        out_shape=(jax.ShapeDtypeStruct((b, n_mem, d), BF16),) * 2,
        grid=(b,),
        in_specs=[blk, _resident((1, d)), _resident(w.shape), _resident((1, d // MEM_HEADS))],
        out_specs=(blk, blk),
        compiler_params=_params("parallel"),
        name="mem_kv",
    )(mem, g, w, kg)


def _mix_mem_kernel(x_ref, ya_ref, yh_ref, wo_ref, g_ref, wq_ref, qg_ref, km_ref, vm_ref, wmo_ref,
                    o_ref):
    d = x_ref.shape[-1]
    hd = d // MEM_HEADS
    y = _dot(ya_ref[...], wo_ref[:ATT_W, :]) + _dot(yh_ref[...], wo_ref[ATT_W:, :])
    x1 = x_ref[...] + y
    h = _rms(x1, g_ref[...]).astype(BF16)
    scale = 1.0 / math.sqrt(hd)
    heads = []
    for hi in range(MEM_HEADS):
        cs = slice(hi * hd, (hi + 1) * hd)
        qh = (_rms(_dot(h, wq_ref[:, cs]), qg_ref[...]) * scale).astype(BF16)
        s = _dot_nt(qh, km_ref[:, cs])
        m = jnp.max(s, axis=-1, keepdims=True)
        p = jnp.exp(s - m)
        l = jnp.sum(p, axis=-1, keepdims=True)
        heads.append((_dot(p.astype(BF16), vm_ref[:, cs]) / l).astype(BF16))
    om = jnp.concatenate(heads, axis=-1)
    o_ref[...] = x1 + _dot(om, wmo_ref[...])


def _mix_mem(x2d, ya, yh, wo, g, wq, qg, km, vm, wmo, tm, seq):
    n, d = x2d.shape
    n_mem = km.shape[1]
    tiles_per_seq = seq // tm
    tok = lambda w: pl.BlockSpec((tm, w), lambda i: (i, 0))
    mem_blk = pl.BlockSpec((None, n_mem, d), lambda i: (i // tiles_per_seq, 0, 0))
    return pl.pallas_call(
        _mix_mem_kernel,
        out_shape=jax.ShapeDtypeStruct((n, d), F32),
        grid=(n // tm,),
        in_specs=[tok(d), tok(ATT_W), tok(HG_W), _resident(wo.shape), _resident((1, d)),
                  _resident(wq.shape), _resident((1, d // MEM_HEADS)), mem_blk, mem_blk,
                  _resident(wmo.shape)],
        out_specs=tok(d),
        compiler_params=_params("parallel"),
        name="mix_mem",
    )(x2d, ya, yh, wo, g, wq, qg, km, vm, wmo)


def _ffn_kernel(x_ref, g_ref, w1_ref, w2_ref, o_ref):
    x = x_ref[...]
    h = _rms(x, g_ref[...]).astype(BF16)
    d_ff = w1_ref.shape[1]
    step = 1024
    acc = x
    for c in range(0, d_ff, step):
        u = jnp.maximum(_dot(h, w1_ref[:, c:c + step]), 0.0)
        acc = acc + _dot((u * u).astype(BF16), w2_ref[c:c + step, :])
    o_ref[...] = acc


def _ffn(x2d, g, w1, w2, tm):
    n, d = x2d.shape
    tok = pl.BlockSpec((tm, d), lambda i: (i, 0))
    return pl.pallas_call(
        _ffn_kernel,
        out_shape=jax.ShapeDtypeStruct((n, d), F32),
        grid=(n // tm,),
        in_specs=[tok, _resident((1, d)), _resident(w1.shape), _resident(w2.shape)],
        out_specs=tok,
        compiler_params=_params("parallel"),
        name="ffn",
    )(x2d, g, w1, w2)


def kernel(x, mem, norm1_g, w_in, attn_qn_g, attn_kn_g, hg_lb, hg_onorm_g, w_out, norm2_g,
           mem_norm_g, w_mq, w_mkv, mq_norm_g, mk_norm_g, w_mo, norm3_g, w_ff1, w_ff2):
    bsz, seq, d = x.shape
    depth = w_in.shape[0]
    tm = 512
    assert seq % tm == 0 and seq % HG_CHUNK == 0

    p_lb = jax.nn.softmax(hg_lb.astype(F32), axis=0)
    lower_bounds = jnp.cumsum(p_lb, axis=0) - p_lb[0:1]
    bias = _attn_bias_tables()
    hg_tables = _hgrn_tables()
    row = lambda v: v.reshape(1, -1).astype(F32)

    x2d = x.reshape(bsz * seq, d)
    for l in range(depth):
        z = _in_proj(x2d, row(norm1_g[l]), w_in[l].astype(BF16), tm).reshape(bsz, seq, -1)
        ya = _dilated_attention(z, attn_qn_g[l].astype(F32), attn_kn_g[l].astype(F32), bias)
        yh = _hgrn(z, row(lower_bounds[l]), row(hg_onorm_g[l]), hg_tables, tm)
        km, vm = _mem_kv(mem, row(mem_norm_g[l]), w_mkv[l].astype(BF16), row(mk_norm_g[l]))
        x2d = _mix_mem(x2d, ya.reshape(bsz * seq, ATT_W), yh.reshape(bsz * seq, HG_W),
                       w_out[l].astype(BF16), row(norm2_g[l]), w_mq[l].astype(BF16),
                       row(mq_norm_g[l]), km, vm, w_mo[l].astype(BF16), tm, seq)
        x2d = _ffn(x2d, row(norm3_g[l]), w_ff1[l].astype(BF16), w_ff2[l].astype(BF16), tm)
    return x2d.reshape(bsz, seq, d)
```

```python
import math

import jax
import jax.numpy as jnp
import numpy as np
from jax import lax
from jax.experimental import pallas as pl
from jax.experimental.pallas import tpu as pltpu

F32 = jnp.float32
BF16 = jnp.bfloat16

EPS = 1e-6
NEG = -1e30
F_MIN = 1e-12
LOG2E = math.log2(math.e)

ATT_HEADS = 8
ATT_HEAD_DIM = 64
ATT_W = ATT_HEADS * ATT_HEAD_DIM
DILATED_PATTERNS = ((128, 1), (512, 4), (2048, 16))
BLK = 128
HG_HEADS = 4
HG_HEAD_DIM = 128
HG_W = HG_HEADS * HG_HEAD_DIM
MEM_HEADS = 4

LANES = 128
SUBLANES = 8
V7X_VMEM_BYTES = 64 * 1024 * 1024
VMEM_LIMIT = V7X_VMEM_BYTES - 8 * 1024 * 1024

IN_PIECE = 256
ATTN_GROUP = 16
MIX_SLABS = 2
HG_CHUNK = 128
HG_GROUP = 4
HG_LEVELS = tuple(1 << i for i in range(int(math.log2(HG_CHUNK))))


def _rms(x, g):
    return x * lax.rsqrt(jnp.mean(x * x, axis=-1, keepdims=True) + EPS) * g


def _sigmoid(x):
    return 1.0 / (1.0 + jnp.exp(-x))


def _dot(a, b):
    return jnp.dot(a, b, preferred_element_type=F32)


def _dot_nt(a, b):
    return lax.dot_general(a, b, (((1,), (1,)), ((), ())), preferred_element_type=F32)


def _dot_tn(a, b):
    return lax.dot_general(a, b, (((0,), (0,)), ((), ())), preferred_element_type=F32)


def _params(*semantics):
    return pltpu.CompilerParams(dimension_semantics=semantics, vmem_limit_bytes=VMEM_LIMIT)


def _resident(shape):
    nd = len(shape)
    return pl.BlockSpec(shape, lambda *_: (0,) * nd, pipeline_mode=pl.Buffered(1))


def _in_proj_kernel(x_ref, g_ref, w_ref, qg_ref, kg_ref, lb_ref,
                    qkv_ref, hq_ref, lf_ref, kk_ref, iv_ref, gs_ref):
    h = _rms(x_ref[...], g_ref[...]).astype(BF16)
    lane = lax.broadcasted_iota(jnp.int32, (1, LANES), 1)
    first = lane < ATT_HEAD_DIM

    def pieces(c):
        for j in range(ATT_W // IN_PIECE):
            cs = slice(j * IN_PIECE, (j + 1) * IN_PIECE)
            yield cs, _dot(h, w_ref[:, c * ATT_W + j * IN_PIECE:c * ATT_W + (j + 1) * IN_PIECE])

    def head_norm(c, gain_ref):
        for cs, z in pieces(c):
            for j in range(IN_PIECE // LANES):
                x = z[:, j * LANES:(j + 1) * LANES]
                x2 = x * x
                sa = jnp.sum(jnp.where(first, x2, 0.0), axis=-1, keepdims=True)
                sb = jnp.sum(jnp.where(first, 0.0, x2), axis=-1, keepdims=True)
                ms = jnp.where(first, sa, sb) * (1.0 / ATT_HEAD_DIM)
                lo = cs.start + j * LANES
                qkv_ref[:, c * ATT_W + lo:c * ATT_W + lo + LANES] = (
                    x * lax.rsqrt(ms + EPS) * gain_ref[:, lo:lo + LANES])

    for cs, z in pieces(4):
        lbound = lb_ref[:, cs]
        sg = _sigmoid(z)
        f = lbound + (1.0 - lbound) * sg
        lf_ref[:, cs] = jnp.log2(jnp.maximum(f, F_MIN))
        kk_ref[:, cs] = ((1.0 - lbound) * (1.0 - sg)).astype(BF16)
    for cs, z in pieces(3):
        hq_ref[:, cs] = (z * _sigmoid(z)).astype(BF16)
    for cs, z in pieces(6):
        gs_ref[:, cs] = (z * _sigmoid(z)).astype(BF16)
    head_norm(0, qg_ref)
    head_norm(1, kg_ref)
    for cs, z in pieces(5):
        iv_ref[:, cs] = z.astype(BF16)
    for cs, z in pieces(2):
        qkv_ref[:, 2 * ATT_W + cs.start:2 * ATT_W + cs.stop] = z


def _in_proj(x2d, g, w, qg, kg, lbound, tm):
    n, d = x2d.shape
    assert w.shape[1] == 3 * ATT_W + 4 * HG_W and ATT_W == HG_W
    tok = lambda width: pl.BlockSpec((tm, width), lambda i: (i, 0))
    hg = lambda dt: jax.ShapeDtypeStruct((n, HG_W), dt)
    return pl.pallas_call(
        _in_proj_kernel,
        out_shape=(jax.ShapeDtypeStruct((n, 3 * ATT_W), F32),
                   hg(BF16), hg(F32), hg(BF16), hg(BF16), hg(BF16)),
        grid=(n // tm,),
        in_specs=[tok(d), _resident((1, d)), _resident(w.shape),
                  _resident((1, ATT_W)), _resident((1, ATT_W)), _resident((1, HG_W))],
        out_specs=(tok(3 * ATT_W), tok(HG_W), tok(HG_W), tok(HG_W), tok(HG_W), tok(HG_W)),
        compiler_params=_params("parallel"),
        name="in_proj",
    )(x2d, g, w, qg, kg, lbound)


def _attn_bias_tables():
    slopes = 2.0 ** (-8.0 / ATT_HEADS * np.arange(1, ATT_HEADS + 1))
    qi = np.arange(BLK)[:, None]
    kj = np.arange(2 * BLK)[None, :]
    step = qi + BLK - kj
    tables = []
    for window, dilation in DILATED_PATTERNS:
        steps = window // dilation
        assert steps == BLK
        valid = (step >= 0) & (step <= steps)
        bias = -slopes[:, None, None] * (step * dilation).astype(np.float64)[None] * LOG2E
        t = np.where(valid[None], bias, NEG)
        tables.append(t.reshape(ATT_HEADS // 2, 2 * BLK, 2 * BLK))
    return jnp.asarray(np.stack(tables), F32)


def _attn_kernel(q_ref, k_ref, v_ref, bias_ref, o_ref,
                 q4, k4, v4, qs, ks, vs, acc4, m4, l4, acc1, m1, l1):
    seq = q_ref.shape[0]
    n_pat = len(DILATED_PATTERNS)
    assert [d for _, d in DILATED_PATTERNS] == [1, 4, 16] and seq == 16 * BLK
    lane = lax.broadcasted_iota(jnp.int32, (1, LANES), 1)
    first = lane < ATT_HEAD_DIM

    def put_operands(pi, blk, q, k, v):
        r = pl.ds(pl.multiple_of(blk * BLK, BLK), BLK)
        qs[pi, blk, :BLK, :] = jnp.where(first, q, 0.0).astype(BF16)
        qs[pi, blk, BLK:, :] = jnp.where(first, 0.0, q).astype(BF16)
        ks[pi, r, :] = k.astype(BF16)
        vs[pi, r, :LANES] = v.astype(BF16)
        vs[pi, r, LANES:] = jnp.ones((BLK, LANES), BF16)

    def copy_body(i, carry):
        r = pl.ds(pl.multiple_of(i * BLK, BLK), BLK)
        put_operands(0, i, q_ref[r, :], k_ref[r, :], v_ref[r, :])
        return carry

    lax.fori_loop(0, seq // BLK, copy_body, 0)

    quarter = seq // 4

    def sort4_body(i, carry):
        res = i // 4
        src = pl.ds(res + (i % 4) * (4 * BLK), BLK, stride=4)
        dst = pl.ds(pl.multiple_of(i * BLK, BLK), BLK)
        q, k, v = q_ref[src, :], k_ref[src, :], v_ref[src, :]
        q4[dst, :] = q
        k4[dst, :] = k
        v4[dst, :] = v
        put_operands(1, i, q, k, v)
        return carry

    lax.fori_loop(0, seq // BLK, sort4_body, 0)

    def sort16_body(i, carry):
        src = pl.ds((i % 4) * quarter + i // 4, BLK, stride=4)
        put_operands(2, i, q4[src, :], k4[src, :], v4[src, :])
        return carry

    lax.fori_loop(0, seq // BLK, sort16_body, 0)

    def key_rows(blk, with_prev):
        if with_prev:
            return pl.ds(pl.multiple_of((blk - 1) * BLK, BLK), 2 * BLK)
        return pl.ds(pl.multiple_of(blk * BLK, BLK), BLK)

    def scores(pi, blk, with_prev):
        bias = bias_ref[pi] if with_prev else bias_ref[pi, :, BLK:]
        return _dot_nt(qs[pi, blk], ks[pi, key_rows(blk, with_prev), :]) + bias

    def block(pi, blk, with_prev, s):
        keys = key_rows(blk, with_prev)
        m = jnp.max(s, axis=-1, keepdims=True)
        p = jnp.exp2(s - m).astype(BF16)
        r = _dot(p, vs[pi, keys, :])
        acc = jnp.where(first, r[:BLK, :LANES], r[BLK:, :LANES])
        l = jnp.where(first, r[:BLK, LANES:], r[BLK:, LANES:])
        mm = jnp.where(first, m[:BLK], m[BLK:])
        return acc, mm, l

    def merge(a, b):
        m = jnp.maximum(a[1], b[1])
        wa = jnp.exp2(a[1] - m)
        wb = jnp.exp2(b[1] - m)
        return wa * a[0] + wb * b[0], m, wa * a[2] + wb * b[2]

    n_groups = seq // BLK // ATTN_GROUP
    assert ATTN_GROUP % 4 == 0

    def grouped(pi, has_prev, consume, groups=(0, n_groups)):
        def group_body(g, carry):
            s_next = scores(pi, g * ATTN_GROUP, has_prev(0))
            for j in range(ATTN_GROUP):
                s_cur = s_next
                if j + 1 < ATTN_GROUP:
                    s_next = scores(pi, g * ATTN_GROUP + j + 1, has_prev(j + 1))
                consume(g, j, block(pi, g * ATTN_GROUP + j, has_prev(j), s_cur))
            return carry

        if groups[1] - groups[0] == 1:
            group_body(groups[0], 0)
        elif groups[1] > groups[0]:
            lax.fori_loop(groups[0], groups[1], group_body, 0)

    def p16(g, j, res):
        a = g * (ATTN_GROUP // 4) + j // 4
        dst = pl.ds((j % 4) * quarter + a, BLK, stride=4)
        acc4[dst, :], m4[dst, :], l4[dst, :] = res

    grouped(2, lambda j: False, p16)

    def p4(g, j, res):
        blk = g * ATTN_GROUP + j
        r4 = g * (ATTN_GROUP // 4) + j // 4
        src = pl.ds(pl.multiple_of(blk * BLK, BLK), BLK)
        dst = pl.ds(r4 + (j % 4) * (4 * BLK), BLK, stride=4)
        acc1[dst, :], m1[dst, :], l1[dst, :] = merge(res, (acc4[src, :], m4[src, :], l4[src, :]))

    grouped(1, lambda j: j % 4 > 0, p4)

    def p1(g, j, res):
        src = pl.ds(pl.multiple_of((g * ATTN_GROUP + j) * BLK, BLK), BLK)
        acc, _, l = merge(res, (acc1[src, :], m1[src, :], l1[src, :]))
        o_ref[src, :] = (acc / l).astype(o_ref.dtype)

    grouped(0, lambda j: j > 0, p1, groups=(0, 1))
    grouped(0, lambda j: True, p1, groups=(1, n_groups))


def _dilated_attention(qkv, bias):
    b, s, _ = qkv.shape
    pairs = ATT_W // LANES
    n_pat = len(DILATED_PATTERNS)
    col = lambda off: pl.BlockSpec((None, s, LANES), lambda bi, hp: (bi, 0, off + hp))
    seq_f32 = pltpu.VMEM((s, LANES), F32)
    return pl.pallas_call(
        _attn_kernel,
        out_shape=jax.ShapeDtypeStruct((b, s, ATT_W), BF16),
        grid=(b, pairs),
        in_specs=[
            col(0), col(pairs), col(2 * pairs),
            pl.BlockSpec((n_pat, None, 2 * BLK, 2 * BLK), lambda bi, hp: (0, hp, 0, 0)),
        ],
        out_specs=pl.BlockSpec((None, s, LANES), lambda bi, hp: (bi, 0, hp)),
        scratch_shapes=[
            seq_f32, seq_f32, seq_f32,
            pltpu.VMEM((n_pat, s // BLK, 2 * BLK, LANES), BF16),
            pltpu.VMEM((n_pat, s, LANES), BF16),
            pltpu.VMEM((n_pat, s, 2 * LANES), BF16),
            seq_f32, seq_f32, seq_f32, seq_f32, seq_f32, seq_f32,
        ],
        compiler_params=_params("parallel", "parallel"),
        name="dilated_attn",
    )(qkv, qkv, qkv, bias)


def _hgrn_tables():
    t = np.arange(HG_CHUNK)[:, None]
    s = np.arange(HG_CHUNK)[None, :]
    ltri = (s <= t).astype(np.float32)
    masks, signs = [], []
    for m in HG_LEVELS:
        same_pair = (t // (2 * m)) == (s // (2 * m))
        masks.append((same_pair & (t % (2 * m) >= m) & (s % (2 * m) < m)).astype(np.float32))
        signs.append(np.broadcast_to(np.where(t % (2 * m) < m, 1.0, -1.0), (HG_CHUNK, LANES)))
    signs = np.stack(signs)
    return (jnp.asarray(ltri, BF16), jnp.asarray(np.stack(masks), F32),
            jnp.asarray(signs, F32), jnp.asarray(signs, BF16))


def _pair_reference(b, m):
    c = HG_CHUNK
    if m >= SUBLANES:
        pieces = []
        for p in range(c // (2 * m)):
            i0 = 2 * m * p + m
            pieces.append(jnp.broadcast_to(b[i0:i0 + 1, :], (2 * m, LANES)))
        return jnp.concatenate(pieces, axis=0)
    groups = c // SUBLANES
    b3 = b.reshape(groups, SUBLANES, LANES)
    sub = lax.broadcasted_iota(jnp.int32, (groups, SUBLANES, LANES), 1)
    ref = None
    for p in range(SUBLANES // (2 * m)):
        i0 = 2 * m * p + m
        piece = jnp.broadcast_to(b3[:, i0:i0 + 1, :], (groups, SUBLANES, LANES))
        ref = piece if ref is None else jnp.where(sub >= 2 * m * p, piece, ref)
    return ref.reshape(c, LANES)


def _hgrn_group(q, lf, kk, vb, states, ltri, lmask_ref, lsign_ref, lsign16_ref):
    heads = range(len(q))
    chunks = range(len(q[0]))
    grid = lambda fn: [[fn(h, c) for c in chunks] for h in heads]

    hi = grid(lambda h, c: lf[h][c].astype(BF16))
    lo = grid(lambda h, c: (lf[h][c] - hi[h][c].astype(F32)).astype(BF16))
    b = grid(lambda h, c: _dot(ltri, hi[h][c]) + _dot(ltri, lo[h][c]))

    a = grid(lambda h, c: jnp.zeros((HG_CHUNK, HG_CHUNK), F32))
    for li, m in enumerate(HG_LEVELS):
        def level_operand(h, c):
            e = jnp.exp2((_pair_reference(b[h][c], m) - b[h][c]) * lsign_ref[li])
            return jnp.where(lsign16_ref[li] > 0, kk[h][c], q[h][c]) * e.astype(BF16)

        xs = grid(level_operand)
        p = grid(lambda h, c: _dot_nt(xs[h][c], xs[h][c]))
        a = grid(lambda h, c: a[h][c] + lmask_ref[li] * p[h][c])
    o = grid(lambda h, c: _dot(a[h][c].astype(BF16), vb[h][c]))
    diag = grid(lambda h, c: jnp.sum((q[h][c] * kk[h][c]).astype(F32), axis=-1, keepdims=True))
    o = grid(lambda h, c: o[h][c] + diag[h][c] * vb[h][c].astype(F32))

    b_last = grid(lambda h, c: b[h][c][HG_CHUNK - 1:HG_CHUNK, :])
    k_end = grid(lambda h, c: kk[h][c] * jnp.exp2(b_last[h][c] - b[h][c]).astype(BF16))
    own = grid(lambda h, c: _dot_tn(vb[h][c], k_end[h][c]))
    carried = []
    new_states = []
    for h in heads:
        st = states[h]
        row = []
        for c in chunks:
            row.append(st)
            st = st * jnp.exp2(b_last[h][c]) + own[h][c]
        carried.append(row)
        new_states.append(st)
    qd = grid(lambda h, c: q[h][c] * jnp.exp2(b[h][c]).astype(BF16))
    o = grid(lambda h, c: o[h][c] + _dot_nt(qd[h][c], carried[h][c].astype(BF16)))
    return o, new_states


def _hgrn_kernel(q_ref, lf_ref, kk_ref, i_ref, gs_ref, og_ref, ltri_ref, lmask_ref, lsign_ref,
                 lsign16_ref, o_ref, st_ref):
    @pl.when(pl.program_id(2) == 0)
    def _():
        st_ref[...] = jnp.zeros_like(st_ref)

    n_heads = q_ref.shape[1] // LANES
    n_groups = q_ref.shape[0] // (HG_CHUNK * HG_GROUP)
    ltri = ltri_ref[...]

    def group_body(gi, carry):
        def rows(c):
            return pl.ds(pl.multiple_of((gi * HG_GROUP + c) * HG_CHUNK, HG_CHUNK), HG_CHUNK)

        cols = [slice(h * LANES, (h + 1) * LANES) for h in range(n_heads)]
        load = lambda ref: [[ref[rows(c), cs] for c in range(HG_GROUP)] for cs in cols]
        o, states = _hgrn_group(load(q_ref), load(lf_ref), load(kk_ref), load(i_ref),
                                [st_ref[h] for h in range(n_heads)],
                                ltri, lmask_ref, lsign_ref, lsign16_ref)
        for h, cs in enumerate(cols):
            st_ref[h] = states[h]
            for c in range(HG_GROUP):
                o_ref[rows(c), cs] = _rms(o[h][c], og_ref[:, cs]).astype(BF16) * gs_ref[rows(c), cs]
        return carry

    lax.fori_loop(0, n_groups, group_body, 0)


def _hgrn(hq, lf, kk, iv, gs, onorm_g, tables, ts):
    b, s, _ = hq.shape
    width = 2 * LANES
    pairs = HG_W // width
    col = pl.BlockSpec((None, ts, width), lambda bi, hp, si: (bi, si, hp))
    par = pl.BlockSpec((1, width), lambda bi, hp, si: (0, hp))
    return pl.pallas_call(
        _hgrn_kernel,
        out_shape=jax.ShapeDtypeStruct((b, s, HG_W), BF16),
        grid=(b, pairs, s // ts),
        in_specs=[col, col, col, col, col, par] + [_resident(t.shape) for t in tables],
        out_specs=col,
        scratch_shapes=[pltpu.VMEM((2, HG_HEAD_DIM, HG_HEAD_DIM), F32)],
        compiler_params=_params("parallel", "parallel", "arbitrary"),
        name="hgrn2",
    )(hq, lf, kk, iv, gs, onorm_g, *tables)


def _mem_kv_kernel(mem_ref, g_ref, w_ref, kg_ref, km_ref, vm_ref):
    d = mem_ref.shape[-1]
    hd = d // MEM_HEADS
    mn = _rms(mem_ref[...], g_ref[...]).astype(BF16)
    for h in range(MEM_HEADS):
        k = _dot(mn, w_ref[:, h * hd:(h + 1) * hd])
        km_ref[:, h * hd:(h + 1) * hd] = _rms(k, kg_ref[...]).astype(BF16)
    vm_ref[...] = _dot(mn, w_ref[:, d:]).astype(BF16)


def _mem_kv(mem, g, w, kg):
    b, n_mem, d = mem.shape
    blk = pl.BlockSpec((None, n_mem, d), lambda bi: (bi, 0, 0))
    return pl.pallas_call(
        _mem_kv_kernel,
        out_shape=(jax.ShapeDtypeStruct((b, n_mem, d), BF16),) * 2,
        grid=(b,),
        in_specs=[blk, _resident((1, d)), _resident(w.shape), _resident((1, d // MEM_HEADS))],
        out_specs=(blk, blk),
        compiler_params=_params("parallel"),
        name="mem_kv",
    )(mem, g, w, kg)


def _mix_mem_kernel(x_ref, ya_ref, yh_ref, wo_ref, g_ref, wq_ref, qg_ref, km_ref, vm_ref, wmo_ref,
                    o_ref):
    tm, d = x_ref.shape
    hd = d // MEM_HEADS
    scale = 1.0 / math.sqrt(hd)
    rows = tm // MIX_SLABS
    slabs = [slice(part * rows, (part + 1) * rows) for part in range(MIX_SLABS)]
    each = lambda fn, *lists: [fn(*args) for args in zip(*lists)]

    y = each(lambda r: _dot(ya_ref[r, :], wo_ref[:ATT_W, :]) + _dot(yh_ref[r, :], wo_ref[ATT_W:, :]),
             slabs)
    x1 = each(lambda r, yy: x_ref[r, :] + yy, slabs, y)
    h = each(lambda xx: _rms(xx, g_ref[...]).astype(BF16), x1)
    heads = [[] for _ in slabs]
    for hi in range(MEM_HEADS):
        cs = slice(hi * hd, (hi + 1) * hd)
        q = each(lambda hh: _dot(hh, wq_ref[:, cs]), h)
        qh = each(lambda qq: (_rms(qq, qg_ref[...]) * scale).astype(BF16), q)
        s = each(lambda qq: _dot_nt(qq, km_ref[:, cs]), qh)
        m = each(lambda ss: jnp.max(ss, axis=-1, keepdims=True), s)
        p = each(lambda ss, mm: jnp.exp(ss - mm), s, m)
        l = each(lambda pp: jnp.sum(pp, axis=-1, keepdims=True), p)
        o = each(lambda pp: _dot(pp.astype(BF16), vm_ref[:, cs]), p)
        for part, (oo, ll) in enumerate(zip(o, l)):
            heads[part].append((oo / ll).astype(BF16))
    om = each(lambda hs: jnp.concatenate(hs, axis=-1), heads)
    out = each(lambda xx, oo: xx + _dot(oo, wmo_ref[...]), x1, om)
    for r, oo in zip(slabs, out):
        o_ref[r, :] = oo


def _mix_mem(x2d, ya, yh, wo, g, wq, qg, km, vm, wmo, tm, seq):
    n, d = x2d.shape
    n_mem = km.shape[1]
    tiles_per_seq = seq // tm
    tok = lambda w: pl.BlockSpec((tm, w), lambda i: (i, 0))
    mem_blk = pl.BlockSpec((None, n_mem, d), lambda i: (i // tiles_per_seq, 0, 0))
    return pl.pallas_call(
        _mix_mem_kernel,
        out_shape=jax.ShapeDtypeStruct((n, d), F32),
        grid=(n // tm,),
        in_specs=[tok(d), tok(ATT_W), tok(HG_W), _resident(wo.shape), _resident((1, d)),
                  _resident(wq.shape), _resident((1, d // MEM_HEADS)), mem_blk, mem_blk,
                  _resident(wmo.shape)],
        out_specs=tok(d),
        compiler_params=_params("parallel"),
        name="mix_mem",
    )(x2d, ya, yh, wo, g, wq, qg, km, vm, wmo)


def _ffn_kernel(x_ref, g_ref, w1_ref, w2_ref, o_ref):
    x = x_ref[...]
    h = _rms(x, g_ref[...]).astype(BF16)
    d_ff = w1_ref.shape[1]
    step = 1024
    acc = x
    for c in range(0, d_ff, step):
        u = jnp.maximum(_dot(h, w1_ref[:, c:c + step]), 0.0)
        acc = acc + _dot((u * u).astype(BF16), w2_ref[c:c + step, :])
    o_ref[...] = acc


def _ffn(x2d, g, w1, w2, tm):
    n, d = x2d.shape
    tok = pl.BlockSpec((tm, d), lambda i: (i, 0))
    return pl.pallas_call(
        _ffn_kernel,
        out_shape=jax.ShapeDtypeStruct((n, d), F32),
        grid=(n // tm,),
        in_specs=[tok, _resident((1, d)), _resident(w1.shape), _resident(w2.shape)],
        out_specs=tok,
        compiler_params=_params("parallel"),
        name="ffn",
    )(x2d, g, w1, w2)


def kernel(x, mem, norm1_g, w_in, attn_qn_g, attn_kn_g, hg_lb, hg_onorm_g, w_out, norm2_g,
           mem_norm_g, w_mq, w_mkv, mq_norm_g, mk_norm_g, w_mo, norm3_g, w_ff1, w_ff2):
    bsz, seq, d = x.shape
    depth = w_in.shape[0]
    tm = 512
    assert seq % tm == 0 and seq % HG_CHUNK == 0

    p_lb = jax.nn.softmax(hg_lb.astype(F32), axis=0)
    lower_bounds = jnp.cumsum(p_lb, axis=0) - p_lb[0:1]
    bias = _attn_bias_tables()
    hg_tables = _hgrn_tables()
    row = lambda v: v.reshape(1, -1).astype(F32)

    x2d = x.reshape(bsz * seq, d)
    for l in range(depth):
        qg = jnp.tile(attn_qn_g[l].astype(F32), ATT_HEADS) * (LOG2E / math.sqrt(ATT_HEAD_DIM))
        kg = jnp.tile(attn_kn_g[l].astype(F32), ATT_HEADS)
        streams = _in_proj(x2d, row(norm1_g[l]), w_in[l].astype(BF16), row(qg), row(kg),
                           row(lower_bounds[l]), tm)
        qkv, hq, lf, kk, iv, gs = (t.reshape(bsz, seq, -1) for t in streams)
        ya = _dilated_attention(qkv, bias)
        yh = _hgrn(hq, lf, kk, iv, gs, row(hg_onorm_g[l]), hg_tables, tm)
        km, vm = _mem_kv(mem, row(mem_norm_g[l]), w_mkv[l].astype(BF16), row(mk_norm_g[l]))
        x2d = _mix_mem(x2d, ya.reshape(bsz * seq, ATT_W), yh.reshape(bsz * seq, HG_W),
                       w_out[l].astype(BF16), row(norm2_g[l]), w_mq[l].astype(BF16),
                       row(mq_norm_g[l]), km, vm, w_mo[l].astype(BF16), tm * MIX_SLABS, seq)
        x2d = _ffn(x2d, row(norm3_g[l]), w_ff1[l].astype(BF16), w_ff2[l].astype(BF16), tm)
    return x2d.reshape(bsz, seq, d)
```

```python
import math

import jax
import jax.numpy as jnp
import numpy as np
from jax import lax
from jax.experimental import pallas as pl
from jax.experimental.pallas import tpu as pltpu

F32 = jnp.float32
BF16 = jnp.bfloat16

EPS = 1e-6
NEG = -1e30
F_MIN = 1e-12
LOG2E = math.log2(math.e)

ATT_HEADS = 8
ATT_HEAD_DIM = 64
ATT_W = ATT_HEADS * ATT_HEAD_DIM
DILATED_PATTERNS = ((128, 1), (512, 4), (2048, 16))
BLK = 128
HG_HEADS = 4
HG_HEAD_DIM = 128
HG_W = HG_HEADS * HG_HEAD_DIM
MEM_HEADS = 4

LANES = 128
SUBLANES = 8
V7X_VMEM_BYTES = 64 * 1024 * 1024
VMEM_LIMIT = V7X_VMEM_BYTES - 8 * 1024 * 1024

IN_PIECE = 256
ATTN_GROUP = 16
MIX_SLABS = 2
HG_CHUNK = 128
HG_GROUP = 2
HG_LEVELS = tuple(1 << i for i in range(int(math.log2(HG_CHUNK))))


def _rms(x, g):
    return x * lax.rsqrt(jnp.mean(x * x, axis=-1, keepdims=True) + EPS) * g


def _sigmoid(x):
    return 1.0 / (1.0 + jnp.exp(-x))


def _dot(a, b):
    return jnp.dot(a, b, preferred_element_type=F32)


def _dot_nt(a, b):
    return lax.dot_general(a, b, (((1,), (1,)), ((), ())), preferred_element_type=F32)


def _dot_tn(a, b):
    return lax.dot_general(a, b, (((0,), (0,)), ((), ())), preferred_element_type=F32)


def _params(*semantics):
    return pltpu.CompilerParams(dimension_semantics=semantics, vmem_limit_bytes=VMEM_LIMIT)


def _resident(shape):
    nd = len(shape)
    return pl.BlockSpec(shape, lambda *_: (0,) * nd, pipeline_mode=pl.Buffered(1))


def _in_proj_kernel(x_ref, g_ref, w_ref, qg_ref, kg_ref, lb_ref,
                    qkv_ref, hq_ref, lf_ref, kk_ref, iv_ref, gs_ref):
    h = _rms(x_ref[...], g_ref[...]).astype(BF16)
    lane = lax.broadcasted_iota(jnp.int32, (1, LANES), 1)
    first = lane < ATT_HEAD_DIM
    pairs = ATT_W // LANES

    def pieces(c):
        for j in range(ATT_W // IN_PIECE):
            cs = slice(j * IN_PIECE, (j + 1) * IN_PIECE)
            yield cs, _dot(h, w_ref[:, c * ATT_W + j * IN_PIECE:c * ATT_W + (j + 1) * IN_PIECE])

    def head_norm(c, gain_ref):
        for cs, z in pieces(c):
            for j in range(IN_PIECE // LANES):
                x = z[:, j * LANES:(j + 1) * LANES]
                x2 = x * x
                sa = jnp.sum(jnp.where(first, x2, 0.0), axis=-1, keepdims=True)
                sb = jnp.sum(jnp.where(first, 0.0, x2), axis=-1, keepdims=True)
                ms = jnp.where(first, sa, sb) * (1.0 / ATT_HEAD_DIM)
                lo = cs.start + j * LANES
                qkv_ref[c * pairs + lo // LANES] = (
                    x * lax.rsqrt(ms + EPS) * gain_ref[:, lo:lo + LANES])

    for cs, z in pieces(4):
        lbound = lb_ref[:, cs]
        sg = _sigmoid(z)
        f = lbound + (1.0 - lbound) * sg
        lf_ref[:, cs] = jnp.log2(jnp.maximum(f, F_MIN))
        kk_ref[:, cs] = ((1.0 - lbound) * (1.0 - sg)).astype(BF16)
    for cs, z in pieces(3):
        hq_ref[:, cs] = (z * _sigmoid(z)).astype(BF16)
    for cs, z in pieces(6):
        gs_ref[:, cs] = (z * _sigmoid(z)).astype(BF16)
    head_norm(0, qg_ref)
    head_norm(1, kg_ref)
    for cs, z in pieces(5):
        iv_ref[:, cs] = z.astype(BF16)
    for cs, z in pieces(2):
        for j in range(IN_PIECE // LANES):
            qkv_ref[2 * pairs + cs.start // LANES + j] = z[:, j * LANES:(j + 1) * LANES]


def _in_proj(x2d, g, w, qg, kg, lbound, tm):
    n, d = x2d.shape
    assert w.shape[1] == 3 * ATT_W + 4 * HG_W and ATT_W == HG_W
    tok = lambda width: pl.BlockSpec((tm, width), lambda i: (i, 0))
    hg = lambda dt: jax.ShapeDtypeStruct((n, HG_W), dt)
    n_qkv = 3 * ATT_W // LANES
    return pl.pallas_call(
        _in_proj_kernel,
        out_shape=(jax.ShapeDtypeStruct((n_qkv, n, LANES), F32),
                   hg(BF16), hg(F32), hg(BF16), hg(BF16), hg(BF16)),
        grid=(n // tm,),
        in_specs=[tok(d), _resident((1, d)), _resident(w.shape),
                  _resident((1, ATT_W)), _resident((1, ATT_W)), _resident((1, HG_W))],
        out_specs=(pl.BlockSpec((n_qkv, tm, LANES), lambda i: (0, i, 0)),
                   tok(HG_W), tok(HG_W), tok(HG_W), tok(HG_W), tok(HG_W)),
        compiler_params=_params("parallel"),
        name="in_proj",
    )(x2d, g, w, qg, kg, lbound)


def _attn_bias_tables():
    slopes = 2.0 ** (-8.0 / ATT_HEADS * np.arange(1, ATT_HEADS + 1))
    qi = np.arange(BLK)[:, None]
    kj = np.arange(2 * BLK)[None, :]
    step = qi + BLK - kj
    tables = []
    for window, dilation in DILATED_PATTERNS:
        steps = window // dilation
        assert steps == BLK
        valid = (step >= 0) & (step <= steps)
        bias = -slopes[:, None, None] * (step * dilation).astype(np.float64)[None] * LOG2E
        t = np.where(valid[None], bias, NEG)
        tables.append(t.reshape(ATT_HEADS // 2, 2 * BLK, 2 * BLK))
    return jnp.asarray(np.stack(tables), F32)


def _attn_kernel(q_ref, k_ref, v_ref, bias_ref, o_ref,
                 q4, k4, v4, qs, ks, vs, acc4, m4, l4, acc1, m1, l1):
    seq = q_ref.shape[0]
    n_pat = len(DILATED_PATTERNS)
    assert [d for _, d in DILATED_PATTERNS] == [1, 4, 16] and seq == 16 * BLK
    lane = lax.broadcasted_iota(jnp.int32, (1, LANES), 1)
    first = lane < ATT_HEAD_DIM

    def put_operands(pi, blk, q, k, v):
        r = pl.ds(pl.multiple_of(blk * BLK, BLK), BLK)
        qs[pi, blk, :BLK, :] = jnp.where(first, q, 0.0).astype(BF16)
        qs[pi, blk, BLK:, :] = jnp.where(first, 0.0, q).astype(BF16)
        ks[pi, r, :] = k.astype(BF16)
        vs[pi, r, :LANES] = v.astype(BF16)

    @pl.when((pl.program_id(0) == 0) & (pl.program_id(1) == 0))
    def _():
        vs[:, :, LANES:] = jnp.ones((n_pat, seq, LANES), BF16)

    def copy_body(i, carry):
        r = pl.ds(pl.multiple_of(i * BLK, BLK), BLK)
        put_operands(0, i, q_ref[r, :], k_ref[r, :], v_ref[r, :])
        return carry

    lax.fori_loop(0, seq // BLK, copy_body, 0)

    quarter = seq // 4

    def sort4_body(i, carry):
        res = i // 4
        src = pl.ds(res + (i % 4) * (4 * BLK), BLK, stride=4)
        dst = pl.ds(pl.multiple_of(i * BLK, BLK), BLK)
        q, k, v = q_ref[src, :], k_ref[src, :], v_ref[src, :]
        q4[dst, :] = q
        k4[dst, :] = k
        v4[dst, :] = v
        put_operands(1, i, q, k, v)
        return carry

    lax.fori_loop(0, seq // BLK, sort4_body, 0)

    def sort16_body(i, carry):
        src = pl.ds((i % 4) * quarter + i // 4, BLK, stride=4)
        put_operands(2, i, q4[src, :], k4[src, :], v4[src, :])
        return carry

    lax.fori_loop(0, seq // BLK, sort16_body, 0)

    def key_rows(blk, with_prev):
        if with_prev:
            return pl.ds(pl.multiple_of((blk - 1) * BLK, BLK), 2 * BLK)
        return pl.ds(pl.multiple_of(blk * BLK, BLK), BLK)

    def scores(pi, blk, with_prev):
        bias = bias_ref[pi] if with_prev else bias_ref[pi, :, BLK:]
        return _dot_nt(qs[pi, blk], ks[pi, key_rows(blk, with_prev), :]) + bias

    def block(pi, blk, with_prev, s):
        keys = key_rows(blk, with_prev)
        m = jnp.max(s, axis=-1, keepdims=True)
        p = jnp.exp2(s - m).astype(BF16)
        r = _dot(p, vs[pi, keys, :])
        acc = jnp.where(first, r[:BLK, :LANES], r[BLK:, :LANES])
        l = jnp.where(first, r[:BLK, LANES:], r[BLK:, LANES:])
        mm = jnp.where(first, m[:BLK], m[BLK:])
        return acc, mm, l

    def merge(a, b):
        m = jnp.maximum(a[1], b[1])
        wa = jnp.exp2(a[1] - m)
        wb = jnp.exp2(b[1] - m)
        return wa * a[0] + wb * b[0], m, wa * a[2] + wb * b[2]

    n_groups = seq // BLK // ATTN_GROUP
    assert ATTN_GROUP % 4 == 0

    def grouped(pi, has_prev, consume, groups=(0, n_groups)):
        def group_body(g, carry):
            s_next = scores(pi, g * ATTN_GROUP, has_prev(0))
            for j in range(ATTN_GROUP):
                s_cur = s_next
                if j + 1 < ATTN_GROUP:
                    s_next = scores(pi, g * ATTN_GROUP + j + 1, has_prev(j + 1))
                consume(g, j, block(pi, g * ATTN_GROUP + j, has_prev(j), s_cur))
            return carry

        if groups[1] - groups[0] == 1:
            group_body(groups[0], 0)
        elif groups[1] > groups[0]:
            lax.fori_loop(groups[0], groups[1], group_body, 0)

    def p16(g, j, res):
        a = g * (ATTN_GROUP // 4) + j // 4
        dst = pl.ds((j % 4) * quarter + a, BLK, stride=4)
        acc4[dst, :], m4[dst, :], l4[dst, :] = res

    grouped(2, lambda j: False, p16)

    def p4(g, j, res):
        blk = g * ATTN_GROUP + j
        r4 = g * (ATTN_GROUP // 4) + j // 4
        src = pl.ds(pl.multiple_of(blk * BLK, BLK), BLK)
        dst = pl.ds(r4 + (j % 4) * (4 * BLK), BLK, stride=4)
        acc1[dst, :], m1[dst, :], l1[dst, :] = merge(res, (acc4[src, :], m4[src, :], l4[src, :]))

    grouped(1, lambda j: j % 4 > 0, p4)

    def p1(g, j, res):
        src = pl.ds(pl.multiple_of((g * ATTN_GROUP + j) * BLK, BLK), BLK)
        acc, _, l = merge(res, (acc1[src, :], m1[src, :], l1[src, :]))
        o_ref[src, :] = (acc / l).astype(o_ref.dtype)

    grouped(0, lambda j: j > 0, p1, groups=(0, 1))
    grouped(0, lambda j: True, p1, groups=(1, n_groups))


def _dilated_attention(qkv, bias):
    _, b, s, _ = qkv.shape
    pairs = ATT_W // LANES
    n_pat = len(DILATED_PATTERNS)
    pair_blk = lambda off: pl.BlockSpec((None, None, s, LANES), lambda bi, hp: (off + hp, bi, 0, 0))
    seq_f32 = pltpu.VMEM((s, LANES), F32)
    return pl.pallas_call(
        _attn_kernel,
        out_shape=jax.ShapeDtypeStruct((pairs, b, s, LANES), BF16),
        grid=(b, pairs),
        in_specs=[
            pair_blk(0), pair_blk(pairs), pair_blk(2 * pairs),
            pl.BlockSpec((n_pat, None, 2 * BLK, 2 * BLK), lambda bi, hp: (0, hp, 0, 0)),
        ],
        out_specs=pair_blk(0),
        scratch_shapes=[
            seq_f32, seq_f32, seq_f32,
            pltpu.VMEM((n_pat, s // BLK, 2 * BLK, LANES), BF16),
            pltpu.VMEM((n_pat, s, LANES), BF16),
            pltpu.VMEM((n_pat, s, 2 * LANES), BF16),
            seq_f32, seq_f32, seq_f32, seq_f32, seq_f32, seq_f32,
        ],
        compiler_params=_params("arbitrary", "arbitrary"),
        name="dilated_attn",
    )(qkv, qkv, qkv, bias)


def _hgrn_tables():
    t = np.arange(HG_CHUNK)[:, None]
    s = np.arange(HG_CHUNK)[None, :]
    ltri = (s <= t).astype(np.float32)
    masks, signs = [], []
    for m in HG_LEVELS:
        same_pair = (t // (2 * m)) == (s // (2 * m))
        masks.append((same_pair & (t % (2 * m) >= m) & (s % (2 * m) < m)).astype(np.float32))
        signs.append(np.broadcast_to(np.where(t % (2 * m) < m, 1.0, -1.0), (HG_CHUNK, LANES)))
    signs = np.stack(signs)
    return (jnp.asarray(ltri, BF16), jnp.asarray(np.stack(masks), F32),
            jnp.asarray(signs, F32), jnp.asarray(signs, BF16))


def _pair_reference(b, m):
    c = HG_CHUNK
    if m >= SUBLANES:
        pieces = []
        for p in range(c // (2 * m)):
            i0 = 2 * m * p + m
            pieces.append(jnp.broadcast_to(b[i0:i0 + 1, :], (2 * m, LANES)))
        return jnp.concatenate(pieces, axis=0)
    groups = c // SUBLANES
    b3 = b.reshape(groups, SUBLANES, LANES)
    sub = lax.broadcasted_iota(jnp.int32, (groups, SUBLANES, LANES), 1)
    ref = None
    for p in range(SUBLANES // (2 * m)):
        i0 = 2 * m * p + m
        piece = jnp.broadcast_to(b3[:, i0:i0 + 1, :], (groups, SUBLANES, LANES))
        ref = piece if ref is None else jnp.where(sub >= 2 * m * p, piece, ref)
    return ref.reshape(c, LANES)


def _hgrn_group(q, lf, kk, vb, states, ltri, lmask_ref, lsign_ref, lsign16_ref):
    heads = range(len(q))
    chunks = range(len(q[0]))
    grid = lambda fn: [[fn(h, c) for c in chunks] for h in heads]

    hi = grid(lambda h, c: lf[h][c].astype(BF16))
    lo = grid(lambda h, c: (lf[h][c] - hi[h][c].astype(F32)).astype(BF16))
    b = grid(lambda h, c: _dot(ltri, hi[h][c]) + _dot(ltri, lo[h][c]))

    a = grid(lambda h, c: jnp.zeros((HG_CHUNK, HG_CHUNK), F32))
    for li, m in enumerate(HG_LEVELS):
        def level_operand(h, c):
            e = jnp.exp2((_pair_reference(b[h][c], m) - b[h][c]) * lsign_ref[li])
            return jnp.where(lsign16_ref[li] > 0, kk[h][c], q[h][c]) * e.astype(BF16)

        xs = grid(level_operand)
        p = grid(lambda h, c: _dot_nt(xs[h][c], xs[h][c]))
        a = grid(lambda h, c: a[h][c] + lmask_ref[li] * p[h][c])
    o = grid(lambda h, c: _dot(a[h][c].astype(BF16), vb[h][c]))
    diag = grid(lambda h, c: jnp.sum((q[h][c] * kk[h][c]).astype(F32), axis=-1, keepdims=True))
    o = grid(lambda h, c: o[h][c] + diag[h][c] * vb[h][c].astype(F32))

    b_last = grid(lambda h, c: b[h][c][HG_CHUNK - 1:HG_CHUNK, :])
    k_end = grid(lambda h, c: kk[h][c] * jnp.exp2(b_last[h][c] - b[h][c]).astype(BF16))
    own = grid(lambda h, c: _dot_tn(vb[h][c], k_end[h][c]))
    carried = []
    new_states = []
    for h in heads:
        st = states[h]
        row = []
        for c in chunks:
            row.append(st)
            st = st * jnp.exp2(b_last[h][c]) + own[h][c]
        carried.append(row)
        new_states.append(st)
    qd = grid(lambda h, c: q[h][c] * jnp.exp2(b[h][c]).astype(BF16))
    o = grid(lambda h, c: o[h][c] + _dot_nt(qd[h][c], carried[h][c].astype(BF16)))
    return o, new_states


def _hgrn_kernel(q_ref, lf_ref, kk_ref, i_ref, gs_ref, og_ref, ltri_ref, lmask_ref, lsign_ref,
                 lsign16_ref, o_ref, st_ref):
    @pl.when(pl.program_id(1) == 0)
    def _():
        st_ref[...] = jnp.zeros_like(st_ref)

    n_heads = q_ref.shape[1] // LANES
    n_groups = q_ref.shape[0] // (HG_CHUNK * HG_GROUP)
    ltri = ltri_ref[...]

    def group_body(gi, carry):
        def rows(c):
            return pl.ds(pl.multiple_of((gi * HG_GROUP + c) * HG_CHUNK, HG_CHUNK), HG_CHUNK)

        cols = [slice(h * LANES, (h + 1) * LANES) for h in range(n_heads)]
        load = lambda ref: [[ref[rows(c), cs] for c in range(HG_GROUP)] for cs in cols]
        o, states = _hgrn_group(load(q_ref), load(lf_ref), load(kk_ref), load(i_ref),
                                [st_ref[h] for h in range(n_heads)],
                                ltri, lmask_ref, lsign_ref, lsign16_ref)
        for h, cs in enumerate(cols):
            st_ref[h] = states[h]
            for c in range(HG_GROUP):
                o_ref[rows(c), cs] = _rms(o[h][c], og_ref[:, cs]).astype(BF16) * gs_ref[rows(c), cs]
        return carry

    lax.fori_loop(0, n_groups, group_body, 0)


def _hgrn(hq, lf, kk, iv, gs, onorm_g, tables, ts):
    b, s, _ = hq.shape
    col = pl.BlockSpec((None, ts, HG_W), lambda bi, si: (bi, si, 0))
    return pl.pallas_call(
        _hgrn_kernel,
        out_shape=jax.ShapeDtypeStruct((b, s, HG_W), BF16),
        grid=(b, s // ts),
        in_specs=[col, col, col, col, col, _resident((1, HG_W))]
        + [_resident(t.shape) for t in tables],
        out_specs=col,
        scratch_shapes=[pltpu.VMEM((HG_HEADS, HG_HEAD_DIM, HG_HEAD_DIM), F32)],
        compiler_params=_params("parallel", "arbitrary"),
        name="hgrn2",
    )(hq, lf, kk, iv, gs, onorm_g, *tables)


def _mem_kv_kernel(mem_ref, g_ref, w_ref, kg_ref, km_ref, vm_ref):
    d = mem_ref.shape[-1]
    hd = d // MEM_HEADS
    mn = _rms(mem_ref[...], g_ref[...]).astype(BF16)
    for h in range(MEM_HEADS):
        k = _dot(mn, w_ref[:, h * hd:(h + 1) * hd])
        km_ref[:, h * hd:(h + 1) * hd] = _rms(k, kg_ref[...]).astype(BF16)
    vm_ref[...] = _dot(mn, w_ref[:, d:]).astype(BF16)


def _mem_kv(mem, g, w, kg):
    b, n_mem, d = mem.shape
    blk = pl.BlockSpec((None, n_mem, d), lambda bi: (bi, 0, 0))
    return pl.pallas_call(
        _mem_kv_kernel,
        out_shape=(jax.ShapeDtypeStruct((b, n_mem, d), BF16),) * 2,
        grid=(b,),
        in_specs=[blk, _resident((1, d)), _resident(w.shape), _resident((1, d // MEM_HEADS))],
        out_specs=(blk, blk),
        compiler_params=_params("parallel"),
        name="mem_kv",
    )(mem, g, w, kg)


def _mix_mem_kernel(x_ref, ya_ref, yh_ref, wo_ref, g_ref, wq_ref, qg_ref, km_ref, vm_ref, wmo_ref,
                    o_ref):
    tm, d = x_ref.shape
    hd = d // MEM_HEADS
    scale = 1.0 / math.sqrt(hd)
    rows = tm // MIX_SLABS
    slabs = [slice(part * rows, (part + 1) * rows) for part in range(MIX_SLABS)]
    each = lambda fn, *lists: [fn(*args) for args in zip(*lists)]

    def mixed(r):
        ya = jnp.concatenate([ya_ref[p, r, :] for p in range(ya_ref.shape[0])], axis=-1)
        return _dot(ya, wo_ref[:ATT_W, :]) + _dot(yh_ref[r, :], wo_ref[ATT_W:, :])

    y = each(mixed, slabs)
    x1 = each(lambda r, yy: x_ref[r, :] + yy, slabs, y)
    h = each(lambda xx: _rms(xx, g_ref[...]).astype(BF16), x1)
    heads = [[] for _ in slabs]
    for hi in range(MEM_HEADS):
        cs = slice(hi * hd, (hi + 1) * hd)
        q = each(lambda hh: _dot(hh, wq_ref[:, cs]), h)
        qh = each(lambda qq: (_rms(qq, qg_ref[...]) * scale).astype(BF16), q)
        s = each(lambda qq: _dot_nt(qq, km_ref[:, cs]), qh)
        m = each(lambda ss: jnp.max(ss, axis=-1, keepdims=True), s)
        p = each(lambda ss, mm: jnp.exp(ss - mm), s, m)
        l = each(lambda pp: jnp.sum(pp, axis=-1, keepdims=True), p)
        o = each(lambda pp: _dot(pp.astype(BF16), vm_ref[:, cs]), p)
        for part, (oo, ll) in enumerate(zip(o, l)):
            heads[part].append((oo / ll).astype(BF16))
    om = each(lambda hs: jnp.concatenate(hs, axis=-1), heads)
    out = each(lambda xx, oo: xx + _dot(oo, wmo_ref[...]), x1, om)
    for r, oo in zip(slabs, out):
        o_ref[r, :] = oo


def _mix_mem(x2d, ya, yh, wo, g, wq, qg, km, vm, wmo, tm, seq):
    n, d = x2d.shape
    n_mem = km.shape[1]
    tiles_per_seq = seq // tm
    tok = lambda w: pl.BlockSpec((tm, w), lambda i: (i, 0))
    mem_blk = pl.BlockSpec((None, n_mem, d), lambda i: (i // tiles_per_seq, 0, 0))
    return pl.pallas_call(
        _mix_mem_kernel,
        out_shape=jax.ShapeDtypeStruct((n, d), F32),
        grid=(n // tm,),
        in_specs=[tok(d), pl.BlockSpec((ya.shape[0], tm, LANES), lambda i: (0, i, 0)), tok(HG_W),
                  _resident(wo.shape), _resident((1, d)),
                  _resident(wq.shape), _resident((1, d // MEM_HEADS)), mem_blk, mem_blk,
                  _resident(wmo.shape)],
        out_specs=tok(d),
        compiler_params=_params("parallel"),
        name="mix_mem",
    )(x2d, ya, yh, wo, g, wq, qg, km, vm, wmo)


def _ffn_kernel(x_ref, g_ref, w1_ref, w2_ref, o_ref):
    x = x_ref[...]
    h = _rms(x, g_ref[...]).astype(BF16)
    d_ff = w1_ref.shape[1]
    step = 1024
    acc = x
    for c in range(0, d_ff, step):
        u = jnp.maximum(_dot(h, w1_ref[:, c:c + step]), 0.0)
        acc = acc + _dot((u * u).astype(BF16), w2_ref[c:c + step, :])
    o_ref[...] = acc


def _ffn(x2d, g, w1, w2, tm):
    n, d = x2d.shape
    tok = pl.BlockSpec((tm, d), lambda i: (i, 0))
    return pl.pallas_call(
        _ffn_kernel,
        out_shape=jax.ShapeDtypeStruct((n, d), F32),
        grid=(n // tm,),
        in_specs=[tok, _resident((1, d)), _resident(w1.shape), _resident(w2.shape)],
        out_specs=tok,
        compiler_params=_params("parallel"),
        name="ffn",
    )(x2d, g, w1, w2)


def kernel(x, mem, norm1_g, w_in, attn_qn_g, attn_kn_g, hg_lb, hg_onorm_g, w_out, norm2_g,
           mem_norm_g, w_mq, w_mkv, mq_norm_g, mk_norm_g, w_mo, norm3_g, w_ff1, w_ff2):
    bsz, seq, d = x.shape
    depth = w_in.shape[0]
    tm = 512
    assert seq % tm == 0 and seq % HG_CHUNK == 0

    p_lb = jax.nn.softmax(hg_lb.astype(F32), axis=0)
    lower_bounds = jnp.cumsum(p_lb, axis=0) - p_lb[0:1]
    bias = _attn_bias_tables()
    hg_tables = _hgrn_tables()
    row = lambda v: v.reshape(1, -1).astype(F32)

    x2d = x.reshape(bsz * seq, d)
    for l in range(depth):
        qg = jnp.tile(attn_qn_g[l].astype(F32), ATT_HEADS) * (LOG2E / math.sqrt(ATT_HEAD_DIM))
        kg = jnp.tile(attn_kn_g[l].astype(F32), ATT_HEADS)
        streams = _in_proj(x2d, row(norm1_g[l]), w_in[l].astype(BF16), row(qg), row(kg),
                           row(lower_bounds[l]), 2 * tm)
        qkv = streams[0].reshape(-1, bsz, seq, LANES)
        hq, lf, kk, iv, gs = (t.reshape(bsz, seq, HG_W) for t in streams[1:])
        ya = _dilated_attention(qkv, bias)
        yh = _hgrn(hq, lf, kk, iv, gs, row(hg_onorm_g[l]), hg_tables, 2 * tm)
        km, vm = _mem_kv(mem, row(mem_norm_g[l]), w_mkv[l].astype(BF16), row(mk_norm_g[l]))
        x2d = _mix_mem(x2d, ya.reshape(-1, bsz * seq, LANES), yh.reshape(bsz * seq, HG_W),
                       w_out[l].astype(BF16), row(norm2_g[l]), w_mq[l].astype(BF16),
                       row(mq_norm_g[l]), km, vm, w_mo[l].astype(BF16), tm * MIX_SLABS, seq)
        x2d = _ffn(x2d, row(norm3_g[l]), w_ff1[l].astype(BF16), w_ff2[l].astype(BF16), tm)
    return x2d.reshape(bsz, seq, d)
```

```python
import math

import jax
import jax.numpy as jnp
import numpy as np
from jax import lax
from jax.experimental import pallas as pl
from jax.experimental.pallas import tpu as pltpu

F32 = jnp.float32
BF16 = jnp.bfloat16

EPS = 1e-6
NEG = -1e30
F_MIN = 1e-12
LOG2E = math.log2(math.e)

ATT_HEADS = 8
ATT_HEAD_DIM = 64
ATT_W = ATT_HEADS * ATT_HEAD_DIM
DILATED_PATTERNS = ((128, 1), (512, 4), (2048, 16))
BLK = 128
HG_HEADS = 4
HG_HEAD_DIM = 128
HG_W = HG_HEADS * HG_HEAD_DIM
MEM_HEADS = 4

LANES = 128
SUBLANES = 8
V7X_VMEM_BYTES = 64 * 1024 * 1024
VMEM_LIMIT = V7X_VMEM_BYTES - 8 * 1024 * 1024

IN_PIECE = 256
ATTN_GROUP = 16
MIX_SLABS = 2
HG_CHUNK = 128
HG_GROUP = 2
HG_LEVELS = tuple(1 << i for i in range(int(math.log2(HG_CHUNK))))


def _rms(x, g):
    return x * lax.rsqrt(jnp.mean(x * x, axis=-1, keepdims=True) + EPS) * g


def _sigmoid(x):
    return 1.0 / (1.0 + jnp.exp(-x))


def _dot(a, b):
    return jnp.dot(a, b, preferred_element_type=F32)


def _dot_nt(a, b):
    return lax.dot_general(a, b, (((1,), (1,)), ((), ())), preferred_element_type=F32)


def _dot_tn(a, b):
    return lax.dot_general(a, b, (((0,), (0,)), ((), ())), preferred_element_type=F32)


def _params(*semantics):
    return pltpu.CompilerParams(dimension_semantics=semantics, vmem_limit_bytes=VMEM_LIMIT)


def _resident(shape):
    nd = len(shape)
    return pl.BlockSpec(shape, lambda *_: (0,) * nd, pipeline_mode=pl.Buffered(1))


def _in_proj_kernel(x_ref, g_ref, w_ref, qg_ref, kg_ref, lb_ref,
                    s1_ref, s4_ref, s16_ref, hq_ref, lf_ref, kk_ref, iv_ref, gs_ref,
                    stage_ref, stage4_ref):
    h = _rms(x_ref[...], g_ref[...]).astype(BF16)
    lane = lax.broadcasted_iota(jnp.int32, (1, LANES), 1)
    first = lane < ATT_HEAD_DIM
    pairs = ATT_W // LANES
    tm = x_ref.shape[0]
    assert [d for _, d in DILATED_PATTERNS] == [1, 4, 16]

    def emit(idx, x):
        slot = idx % 2
        s1_ref[idx, 0] = x.astype(BF16)
        stage_ref[slot] = x
        for r in range(4):
            y = stage_ref[slot, pl.ds(r, tm // 4, stride=4), :]
            s4_ref[idx, r] = y.astype(BF16)
            stage4_ref[slot, r * (tm // 4):(r + 1) * (tm // 4), :] = y
        for a in range(4):
            for r in range(4):
                y = stage4_ref[slot, pl.ds(r * (tm // 4) + a, tm // 16, stride=4), :]
                s16_ref[idx, 4 * a + r] = y.astype(BF16)

    def pieces(c):
        for j in range(ATT_W // IN_PIECE):
            cs = slice(j * IN_PIECE, (j + 1) * IN_PIECE)
            yield cs, _dot(h, w_ref[:, c * ATT_W + j * IN_PIECE:c * ATT_W + (j + 1) * IN_PIECE])

    def attention_operand(c, gain_ref=None):
        for cs, z in pieces(c):
            for j in range(IN_PIECE // LANES):
                lo = cs.start + j * LANES
                x = z[:, j * LANES:(j + 1) * LANES]
                if gain_ref is not None:
                    x2 = x * x
                    sa = jnp.sum(jnp.where(first, x2, 0.0), axis=-1, keepdims=True)
                    sb = jnp.sum(jnp.where(first, 0.0, x2), axis=-1, keepdims=True)
                    ms = jnp.where(first, sa, sb) * (1.0 / ATT_HEAD_DIM)
                    x = x * lax.rsqrt(ms + EPS) * gain_ref[:, lo:lo + LANES]
                emit(c * pairs + lo // LANES, x)

    attention_operand(0, qg_ref)
    attention_operand(1, kg_ref)
    attention_operand(2)
    for cs, z in pieces(4):
        lbound = lb_ref[:, cs]
        sg = _sigmoid(z)
        f = lbound + (1.0 - lbound) * sg
        lf_ref[:, cs] = jnp.log2(jnp.maximum(f, F_MIN))
        kk_ref[:, cs] = ((1.0 - lbound) * (1.0 - sg)).astype(BF16)
    for cs, z in pieces(3):
        hq_ref[:, cs] = (z * _sigmoid(z)).astype(BF16)
    for cs, z in pieces(6):
        gs_ref[:, cs] = (z * _sigmoid(z)).astype(BF16)
    for cs, z in pieces(5):
        iv_ref[:, cs] = z.astype(BF16)


def _in_proj(x2d, g, w, qg, kg, lbound, tm, seq):
    n, d = x2d.shape
    assert w.shape[1] == 3 * ATT_W + 4 * HG_W and ATT_W == HG_W and seq % tm == 0 and tm % 256 == 0
    bsz = n // seq
    tiles = seq // tm
    tok = lambda width: pl.BlockSpec((tm, width), lambda i: (i, 0))
    hg = lambda dt: jax.ShapeDtypeStruct((n, HG_W), dt)
    n_op = 3 * ATT_W // LANES
    sorted_by = lambda dil: (
        jax.ShapeDtypeStruct((n_op, bsz, dil, seq // dil, LANES), BF16),
        pl.BlockSpec((n_op, None, dil, tm // dil, LANES), lambda i: (0, i // tiles, 0, i % tiles, 0)))
    (shape1, spec1), (shape4, spec4), (shape16, spec16) = sorted_by(1), sorted_by(4), sorted_by(16)
    stage = pltpu.VMEM((2, tm, LANES), F32)
    return pl.pallas_call(
        _in_proj_kernel,
        out_shape=(shape1, shape4, shape16, hg(BF16), hg(F32), hg(BF16), hg(BF16), hg(BF16)),
        grid=(n // tm,),
        in_specs=[tok(d), _resident((1, d)), _resident(w.shape),
                  _resident((1, ATT_W)), _resident((1, ATT_W)), _resident((1, HG_W))],
        out_specs=(spec1, spec4, spec16, tok(HG_W), tok(HG_W), tok(HG_W), tok(HG_W), tok(HG_W)),
        scratch_shapes=[stage, stage],
        compiler_params=_params("parallel"),
        name="in_proj",
    )(x2d, g, w, qg, kg, lbound)


def _attn_bias_tables():
    slopes = 2.0 ** (-8.0 / ATT_HEADS * np.arange(1, ATT_HEADS + 1))
    qi = np.arange(BLK)[:, None]
    kj = np.arange(2 * BLK)[None, :]
    step = qi + BLK - kj
    tables = []
    for window, dilation in DILATED_PATTERNS:
        steps = window // dilation
        assert steps == BLK
        valid = (step >= 0) & (step <= steps)
        bias = -slopes[:, None, None] * (step * dilation).astype(np.float64)[None] * LOG2E
        t = np.where(valid[None], bias, NEG)
        tables.append(t.reshape(ATT_HEADS // 2, 2 * BLK, 2 * BLK))
    return jnp.asarray(np.stack(tables), F32)


def _attn_kernel(q1, k1, v1, q4, k4, v4, q16, k16, v16, bias_ref, o_ref,
                 acc4, m4, l4, acc1, m1, l1):
    seq = q1.shape[0]
    qs, ks, vs = (q1, q4, q16), (k1, k4, k16), (v1, v4, v16)
    assert [d for _, d in DILATED_PATTERNS] == [1, 4, 16] and seq == 16 * BLK
    lane = lax.broadcasted_iota(jnp.int32, (1, LANES), 1)
    first = lane < ATT_HEAD_DIM
    quarter = seq // 4
    zero = jnp.zeros((), BF16)
    ones = jnp.ones((2 * BLK, LANES), BF16)

    def key_rows(blk, with_prev):
        if with_prev:
            return pl.ds(pl.multiple_of((blk - 1) * BLK, BLK), 2 * BLK)
        return pl.ds(pl.multiple_of(blk * BLK, BLK), BLK)

    def scores(pi, blk, with_prev):
        q = qs[pi][pl.ds(pl.multiple_of(blk * BLK, BLK), BLK), :]
        q2 = jnp.concatenate([jnp.where(first, q, zero), jnp.where(first, zero, q)], axis=0)
        bias = bias_ref[pi] if with_prev else bias_ref[pi, :, BLK:]
        return _dot_nt(q2, ks[pi][key_rows(blk, with_prev), :]) + bias

    def block(pi, blk, with_prev, s):
        v = vs[pi][key_rows(blk, with_prev), :]
        v2 = jnp.concatenate([v, ones[:v.shape[0]]], axis=-1)
        m = jnp.max(s, axis=-1, keepdims=True)
        p = jnp.exp2(s - m).astype(BF16)
        r = _dot(p, v2)
        acc = jnp.where(first, r[:BLK, :LANES], r[BLK:, :LANES])
        l = jnp.where(first, r[:BLK, LANES:], r[BLK:, LANES:])
        mm = jnp.where(first, m[:BLK], m[BLK:])
        return acc, mm, l

    def merge(a, b):
        m = jnp.maximum(a[1], b[1])
        wa = jnp.exp2(a[1] - m)
        wb = jnp.exp2(b[1] - m)
        return wa * a[0] + wb * b[0], m, wa * a[2] + wb * b[2]

    n_groups = seq // BLK // ATTN_GROUP
    assert ATTN_GROUP % 4 == 0

    def grouped(pi, has_prev, consume, groups=(0, n_groups)):
        def group_body(g, carry):
            s_next = scores(pi, g * ATTN_GROUP, has_prev(0))
            for j in range(ATTN_GROUP):
                s_cur = s_next
                if j + 1 < ATTN_GROUP:
                    s_next = scores(pi, g * ATTN_GROUP + j + 1, has_prev(j + 1))
                consume(g, j, block(pi, g * ATTN_GROUP + j, has_prev(j), s_cur))
            return carry

        if groups[1] - groups[0] == 1:
            group_body(groups[0], 0)
        elif groups[1] > groups[0]:
            lax.fori_loop(groups[0], groups[1], group_body, 0)

    def p16(g, j, res):
        a = g * (ATTN_GROUP // 4) + j // 4
        dst = pl.ds((j % 4) * quarter + a, BLK, stride=4)
        acc4[dst, :], m4[dst, :], l4[dst, :] = res

    grouped(2, lambda j: False, p16)

    def p4(g, j, res):
        blk = g * ATTN_GROUP + j
        r4 = g * (ATTN_GROUP // 4) + j // 4
        src = pl.ds(pl.multiple_of(blk * BLK, BLK), BLK)
        dst = pl.ds(r4 + (j % 4) * (4 * BLK), BLK, stride=4)
        acc1[dst, :], m1[dst, :], l1[dst, :] = merge(res, (acc4[src, :], m4[src, :], l4[src, :]))

    grouped(1, lambda j: j % 4 > 0, p4)

    def p1(g, j, res):
        src = pl.ds(pl.multiple_of((g * ATTN_GROUP + j) * BLK, BLK), BLK)
        acc, _, l = merge(res, (acc1[src, :], m1[src, :], l1[src, :]))
        o_ref[src, :] = (acc / l).astype(o_ref.dtype)

    grouped(0, lambda j: j > 0, p1, groups=(0, 1))
    grouped(0, lambda j: True, p1, groups=(1, n_groups))


def _dilated_attention(operands, bias):
    _, b, s, _ = operands[0].shape
    pairs = ATT_W // LANES
    n_pat = len(DILATED_PATTERNS)
    pair_blk = lambda off: pl.BlockSpec((None, None, s, LANES), lambda bi, hp: (off + hp, bi, 0, 0))
    seq_f32 = pltpu.VMEM((s, LANES), F32)
    qkv_specs = [pair_blk(0), pair_blk(pairs), pair_blk(2 * pairs)]
    return pl.pallas_call(
        _attn_kernel,
        out_shape=jax.ShapeDtypeStruct((pairs, b, s, LANES), BF16),
        grid=(b, pairs),
        in_specs=qkv_specs * n_pat
        + [pl.BlockSpec((n_pat, None, 2 * BLK, 2 * BLK), lambda bi, hp: (0, hp, 0, 0))],
        out_specs=pair_blk(0),
        scratch_shapes=[seq_f32] * 6,
        compiler_params=_params("parallel", "parallel"),
        name="dilated_attn",
    )(*[op for op in operands for _ in range(3)], bias)


def _hgrn_tables():
    t = np.arange(HG_CHUNK)[:, None]
    s = np.arange(HG_CHUNK)[None, :]
    ltri = (s <= t).astype(np.float32)
    masks, signs = [], []
    for m in HG_LEVELS:
        same_pair = (t // (2 * m)) == (s // (2 * m))
        masks.append((same_pair & (t % (2 * m) >= m) & (s % (2 * m) < m)).astype(np.float32))
        signs.append(np.broadcast_to(np.where(t % (2 * m) < m, 1.0, -1.0), (HG_CHUNK, LANES)))
    signs = np.stack(signs)
    return (jnp.asarray(ltri, BF16), jnp.asarray(np.stack(masks), F32),
            jnp.asarray(signs, F32), jnp.asarray(signs, BF16))


def _pair_reference(b, m):
    c = HG_CHUNK
    if m >= SUBLANES:
        pieces = []
        for p in range(c // (2 * m)):
            i0 = 2 * m * p + m
            pieces.append(jnp.broadcast_to(b[i0:i0 + 1, :], (2 * m, LANES)))
        return jnp.concatenate(pieces, axis=0)
    groups = c // SUBLANES
    b3 = b.reshape(groups, SUBLANES, LANES)
    sub = lax.broadcasted_iota(jnp.int32, (groups, SUBLANES, LANES), 1)
    ref = None
    for p in range(SUBLANES // (2 * m)):
        i0 = 2 * m * p + m
        piece = jnp.broadcast_to(b3[:, i0:i0 + 1, :], (groups, SUBLANES, LANES))
        ref = piece if ref is None else jnp.where(sub >= 2 * m * p, piece, ref)
    return ref.reshape(c, LANES)


def _hgrn_group(q, lf, kk, vb, states, ltri, lmask_ref, lsign_ref, lsign16_ref):
    heads = range(len(q))
    chunks = range(len(q[0]))
    grid = lambda fn: [[fn(h, c) for c in chunks] for h in heads]

    hi = grid(lambda h, c: lf[h][c].astype(BF16))
    lo = grid(lambda h, c: (lf[h][c] - hi[h][c].astype(F32)).astype(BF16))
    b = grid(lambda h, c: _dot(ltri, hi[h][c]) + _dot(ltri, lo[h][c]))

    a = grid(lambda h, c: jnp.zeros((HG_CHUNK, HG_CHUNK), F32))
    for li, m in enumerate(HG_LEVELS):
        def level_operand(h, c):
            e = jnp.exp2((_pair_reference(b[h][c], m) - b[h][c]) * lsign_ref[li])
            return jnp.where(lsign16_ref[li] > 0, kk[h][c], q[h][c]) * e.astype(BF16)

        xs = grid(level_operand)
        p = grid(lambda h, c: _dot_nt(xs[h][c], xs[h][c]))
        a = grid(lambda h, c: a[h][c] + lmask_ref[li] * p[h][c])
    o = grid(lambda h, c: _dot(a[h][c].astype(BF16), vb[h][c]))
    diag = grid(lambda h, c: jnp.sum((q[h][c] * kk[h][c]).astype(F32), axis=-1, keepdims=True))
    o = grid(lambda h, c: o[h][c] + diag[h][c] * vb[h][c].astype(F32))

    b_last = grid(lambda h, c: b[h][c][HG_CHUNK - 1:HG_CHUNK, :])
    k_end = grid(lambda h, c: kk[h][c] * jnp.exp2(b_last[h][c] - b[h][c]).astype(BF16))
    own = grid(lambda h, c: _dot_tn(vb[h][c], k_end[h][c]))
    carried = []
    new_states = []
    for h in heads:
        st = states[h]
        row = []
        for c in chunks:
            row.append(st)
            st = st * jnp.exp2(b_last[h][c]) + own[h][c]
        carried.append(row)
        new_states.append(st)
    qd = grid(lambda h, c: q[h][c] * jnp.exp2(b[h][c]).astype(BF16))
    o = grid(lambda h, c: o[h][c] + _dot_nt(qd[h][c], carried[h][c].astype(BF16)))
    return o, new_states


def _hgrn_kernel(q_ref, lf_ref, kk_ref, i_ref, gs_ref, og_ref, ltri_ref, lmask_ref, lsign_ref,
                 lsign16_ref, o_ref, st_ref):
    @pl.when(pl.program_id(1) == 0)
    def _():
        st_ref[...] = jnp.zeros_like(st_ref)

    n_heads = q_ref.shape[1] // LANES
    n_groups = q_ref.shape[0] // (HG_CHUNK * HG_GROUP)
    ltri = ltri_ref[...]

    def group_body(gi, carry):
        def rows(c):
            return pl.ds(pl.multiple_of((gi * HG_GROUP + c) * HG_CHUNK, HG_CHUNK), HG_CHUNK)

        cols = [slice(h * LANES, (h + 1) * LANES) for h in range(n_heads)]
        load = lambda ref: [[ref[rows(c), cs] for c in range(HG_GROUP)] for cs in cols]
        o, states = _hgrn_group(load(q_ref), load(lf_ref), load(kk_ref), load(i_ref),
                                [st_ref[h] for h in range(n_heads)],
                                ltri, lmask_ref, lsign_ref, lsign16_ref)
        for h, cs in enumerate(cols):
            st_ref[h] = states[h]
            for c in range(HG_GROUP):
                o_ref[rows(c), cs] = _rms(o[h][c], og_ref[:, cs]).astype(BF16) * gs_ref[rows(c), cs]
        return carry

    lax.fori_loop(0, n_groups, group_body, 0)


def _hgrn(hq, lf, kk, iv, gs, onorm_g, tables, ts):
    b, s, _ = hq.shape
    col = pl.BlockSpec((None, ts, HG_W), lambda bi, si: (bi, si, 0))
    return pl.pallas_call(
        _hgrn_kernel,
        out_shape=jax.ShapeDtypeStruct((b, s, HG_W), BF16),
        grid=(b, s // ts),
        in_specs=[col, col, col, col, col, _resident((1, HG_W))]
        + [_resident(t.shape) for t in tables],
        out_specs=col,
        scratch_shapes=[pltpu.VMEM((HG_HEADS, HG_HEAD_DIM, HG_HEAD_DIM), F32)],
        compiler_params=_params("parallel", "arbitrary"),
        name="hgrn2",
    )(hq, lf, kk, iv, gs, onorm_g, *tables)


def _mem_kv_kernel(mem_ref, g_ref, w_ref, kg_ref, km_ref, vm_ref):
    d = mem_ref.shape[-1]
    hd = d // MEM_HEADS
    mn = _rms(mem_ref[...], g_ref[...]).astype(BF16)
    for h in range(MEM_HEADS):
        k = _dot(mn, w_ref[:, h * hd:(h + 1) * hd])
        km_ref[:, h * hd:(h + 1) * hd] = _rms(k, kg_ref[...]).astype(BF16)
    vm_ref[...] = _dot(mn, w_ref[:, d:]).astype(BF16)


def _mem_kv(mem, g, w, kg):
    b, n_mem, d = mem.shape
    blk = pl.BlockSpec((None, n_mem, d), lambda bi: (bi, 0, 0))
    return pl.pallas_call(
        _mem_kv_kernel,
        out_shape=(jax.ShapeDtypeStruct((b, n_mem, d), BF16),) * 2,
        grid=(b,),
        in_specs=[blk, _resident((1, d)), _resident(w.shape), _resident((1, d // MEM_HEADS))],
        out_specs=(blk, blk),
        compiler_params=_params("parallel"),
        name="mem_kv",
    )(mem, g, w, kg)


def _mix_mem_kernel(x_ref, ya_ref, yh_ref, wo_ref, g_ref, wq_ref, qg_ref, km_ref, vm_ref, wmo_ref,
                    o_ref):
    tm, d = x_ref.shape
    hd = d // MEM_HEADS
    scale = 1.0 / math.sqrt(hd)
    rows = tm // MIX_SLABS
    slabs = [slice(part * rows, (part + 1) * rows) for part in range(MIX_SLABS)]
    each = lambda fn, *lists: [fn(*args) for args in zip(*lists)]

    def mixed(r):
        ya = jnp.concatenate([ya_ref[p, r, :] for p in range(ya_ref.shape[0])], axis=-1)
        return _dot(ya, wo_ref[:ATT_W, :]) + _dot(yh_ref[r, :], wo_ref[ATT_W:, :])

    y = each(mixed, slabs)
    x1 = each(lambda r, yy: x_ref[r, :] + yy, slabs, y)
    h = each(lambda xx: _rms(xx, g_ref[...]).astype(BF16), x1)
    heads = [[] for _ in slabs]
    for hi in range(MEM_HEADS):
        cs = slice(hi * hd, (hi + 1) * hd)
        q = each(lambda hh: _dot(hh, wq_ref[:, cs]), h)
        qh = each(lambda qq: (_rms(qq, qg_ref[...]) * scale).astype(BF16), q)
        s = each(lambda qq: _dot_nt(qq, km_ref[:, cs]), qh)
        m = each(lambda ss: jnp.max(ss, axis=-1, keepdims=True), s)
        p = each(lambda ss, mm: jnp.exp(ss - mm), s, m)
        l = each(lambda pp: jnp.sum(pp, axis=-1, keepdims=True), p)
        o = each(lambda pp: _dot(pp.astype(BF16), vm_ref[:, cs]), p)
        for part, (oo, ll) in enumerate(zip(o, l)):
            heads[part].append((oo / ll).astype(BF16))
    om = each(lambda hs: jnp.concatenate(hs, axis=-1), heads)
    out = each(lambda xx, oo: xx + _dot(oo, wmo_ref[...]), x1, om)
    for r, oo in zip(slabs, out):
        o_ref[r, :] = oo


def _mix_mem(x2d, ya, yh, wo, g, wq, qg, km, vm, wmo, tm, seq):
    n, d = x2d.shape
    n_mem = km.shape[1]
    tiles_per_seq = seq // tm
    tok = lambda w: pl.BlockSpec((tm, w), lambda i: (i, 0))
    mem_blk = pl.BlockSpec((None, n_mem, d), lambda i: (i // tiles_per_seq, 0, 0))
    return pl.pallas_call(
        _mix_mem_kernel,
        out_shape=jax.ShapeDtypeStruct((n, d), F32),
        grid=(n // tm,),
        in_specs=[tok(d), pl.BlockSpec((ya.shape[0], tm, LANES), lambda i: (0, i, 0)), tok(HG_W),
                  _resident(wo.shape), _resident((1, d)),
                  _resident(wq.shape), _resident((1, d // MEM_HEADS)), mem_blk, mem_blk,
                  _resident(wmo.shape)],
        out_specs=tok(d),
        compiler_params=_params("parallel"),
        name="mix_mem",
    )(x2d, ya, yh, wo, g, wq, qg, km, vm, wmo)


def _ffn_kernel(x_ref, g_ref, w1_ref, w2_ref, o_ref):
    x = x_ref[...]
    h = _rms(x, g_ref[...]).astype(BF16)
    d_ff = w1_ref.shape[1]
    step = 1024
    acc = x
    for c in range(0, d_ff, step):
        u = jnp.maximum(_dot(h, w1_ref[:, c:c + step]), 0.0)
        acc = acc + _dot((u * u).astype(BF16), w2_ref[c:c + step, :])
    o_ref[...] = acc


def _ffn(x2d, g, w1, w2, tm):
    n, d = x2d.shape
    tok = pl.BlockSpec((tm, d), lambda i: (i, 0))
    return pl.pallas_call(
        _ffn_kernel,
        out_shape=jax.ShapeDtypeStruct((n, d), F32),
        grid=(n // tm,),
        in_specs=[tok, _resident((1, d)), _resident(w1.shape), _resident(w2.shape)],
        out_specs=tok,
        compiler_params=_params("parallel"),
        name="ffn",
    )(x2d, g, w1, w2)


def kernel(x, mem, norm1_g, w_in, attn_qn_g, attn_kn_g, hg_lb, hg_onorm_g, w_out, norm2_g,
           mem_norm_g, w_mq, w_mkv, mq_norm_g, mk_norm_g, w_mo, norm3_g, w_ff1, w_ff2):
    bsz, seq, d = x.shape
    depth = w_in.shape[0]
    tm = 512
    assert seq % tm == 0 and seq % HG_CHUNK == 0

    p_lb = jax.nn.softmax(hg_lb.astype(F32), axis=0)
    lower_bounds = jnp.cumsum(p_lb, axis=0) - p_lb[0:1]
    bias = _attn_bias_tables()
    hg_tables = _hgrn_tables()
    row = lambda v: v.reshape(1, -1).astype(F32)

    x2d = x.reshape(bsz * seq, d)
    for l in range(depth):
        qg = jnp.tile(attn_qn_g[l].astype(F32), ATT_HEADS) * (LOG2E / math.sqrt(ATT_HEAD_DIM))
        kg = jnp.tile(attn_kn_g[l].astype(F32), ATT_HEADS)
        streams = _in_proj(x2d, row(norm1_g[l]), w_in[l].astype(BF16), row(qg), row(kg),
                           row(lower_bounds[l]), 2 * tm, seq)
        operands = [t.reshape(-1, bsz, seq, LANES) for t in streams[:3]]
        hq, lf, kk, iv, gs = (t.reshape(bsz, seq, HG_W) for t in streams[3:])
        ya = _dilated_attention(operands, bias)
        yh = _hgrn(hq, lf, kk, iv, gs, row(hg_onorm_g[l]), hg_tables, 2 * tm)
        km, vm = _mem_kv(mem, row(mem_norm_g[l]), w_mkv[l].astype(BF16), row(mk_norm_g[l]))
        x2d = _mix_mem(x2d, ya.reshape(-1, bsz * seq, LANES), yh.reshape(bsz * seq, HG_W),
                       w_out[l].astype(BF16), row(norm2_g[l]), w_mq[l].astype(BF16),
                       row(mq_norm_g[l]), km, vm, w_mo[l].astype(BF16), tm * MIX_SLABS, seq)
        x2d = _ffn(x2d, row(norm3_g[l]), w_ff1[l].astype(BF16), w_ff2[l].astype(BF16), tm)
    return x2d.reshape(bsz, seq, d)
```

```python
import math

import jax
import jax.numpy as jnp
import numpy as np
from jax import lax
from jax.experimental import pallas as pl
from jax.experimental.pallas import tpu as pltpu

F32 = jnp.float32
BF16 = jnp.bfloat16

EPS = 1e-6
NEG = -1e30
F_MIN = 1e-12
LOG2E = math.log2(math.e)

ATT_HEADS = 8
ATT_HEAD_DIM = 64
ATT_W = ATT_HEADS * ATT_HEAD_DIM
DILATED_PATTERNS = ((128, 1), (512, 4), (2048, 16))
BLK = 128
HG_HEADS = 4
HG_HEAD_DIM = 128
HG_W = HG_HEADS * HG_HEAD_DIM
MEM_HEADS = 4

LANES = 128
SUBLANES = 8
V7X_VMEM_BYTES = 64 * 1024 * 1024
VMEM_LIMIT = V7X_VMEM_BYTES - 8 * 1024 * 1024

IN_PIECE = 256
ATTN_GROUP = 16
MIX_SLABS = 2
HG_CHUNK = 128
HG_GROUP = 2
HG_LEVELS = tuple(1 << i for i in range(int(math.log2(HG_CHUNK))))


def _rms(x, g):
    return x * lax.rsqrt(jnp.mean(x * x, axis=-1, keepdims=True) + EPS) * g


def _sigmoid(x):
    return 1.0 / (1.0 + jnp.exp(-x))


def _dot(a, b):
    return jnp.dot(a, b, preferred_element_type=F32)


def _dot_nt(a, b):
    return lax.dot_general(a, b, (((1,), (1,)), ((), ())), preferred_element_type=F32)


def _dot_tn(a, b):
    return lax.dot_general(a, b, (((0,), (0,)), ((), ())), preferred_element_type=F32)


def _params(*semantics):
    return pltpu.CompilerParams(dimension_semantics=semantics, vmem_limit_bytes=VMEM_LIMIT)


def _resident(shape):
    nd = len(shape)
    return pl.BlockSpec(shape, lambda *_: (0,) * nd, pipeline_mode=pl.Buffered(1))


def _in_proj_kernel(x_ref, g_ref, w_ref, qg_ref, kg_ref, lb_ref,
                    s1_ref, s4_ref, s16_ref, hq_ref, lf_ref, kk_ref, iv_ref, gs_ref,
                    stage_ref, stage4_ref):
    h = _rms(x_ref[...], g_ref[...]).astype(BF16)
    lane = lax.broadcasted_iota(jnp.int32, (1, LANES), 1)
    first = lane < ATT_HEAD_DIM
    pairs = ATT_W // LANES
    tm = x_ref.shape[0]
    assert [d for _, d in DILATED_PATTERNS] == [1, 4, 16]

    def emit(idx, x):
        slot = idx % 2
        s1_ref[idx, 0] = x.astype(BF16)
        stage_ref[slot] = x
        for r in range(4):
            y = stage_ref[slot, pl.ds(r, tm // 4, stride=4), :]
            s4_ref[idx, r] = y.astype(BF16)
            stage4_ref[slot, r * (tm // 4):(r + 1) * (tm // 4), :] = y
        for a in range(4):
            for r in range(4):
                y = stage4_ref[slot, pl.ds(r * (tm // 4) + a, tm // 16, stride=4), :]
                s16_ref[idx, 4 * a + r] = y.astype(BF16)

    def pieces(c):
        for j in range(ATT_W // IN_PIECE):
            cs = slice(j * IN_PIECE, (j + 1) * IN_PIECE)
            yield cs, _dot(h, w_ref[:, c * ATT_W + j * IN_PIECE:c * ATT_W + (j + 1) * IN_PIECE])

    def attention_operand(c, gain_ref=None):
        for cs, z in pieces(c):
            for j in range(IN_PIECE // LANES):
                lo = cs.start + j * LANES
                x = z[:, j * LANES:(j + 1) * LANES]
                if gain_ref is not None:
                    x2 = x * x
                    sa = jnp.sum(jnp.where(first, x2, 0.0), axis=-1, keepdims=True)
                    sb = jnp.sum(jnp.where(first, 0.0, x2), axis=-1, keepdims=True)
                    ms = jnp.where(first, sa, sb) * (1.0 / ATT_HEAD_DIM)
                    x = x * lax.rsqrt(ms + EPS) * gain_ref[:, lo:lo + LANES]
                emit(c * pairs + lo // LANES, x)

    attention_operand(0, qg_ref)
    attention_operand(1, kg_ref)
    attention_operand(2)
    for cs, z in pieces(4):
        lbound = lb_ref[:, cs]
        sg = _sigmoid(z)
        f = lbound + (1.0 - lbound) * sg
        lf_ref[:, cs] = jnp.log2(jnp.maximum(f, F_MIN))
        kk_ref[:, cs] = ((1.0 - lbound) * (1.0 - sg)).astype(BF16)
    for cs, z in pieces(3):
        hq_ref[:, cs] = (z * _sigmoid(z)).astype(BF16)
    for cs, z in pieces(6):
        gs_ref[:, cs] = (z * _sigmoid(z)).astype(BF16)
    for cs, z in pieces(5):
        iv_ref[:, cs] = z.astype(BF16)


def _in_proj(x2d, g, w, qg, kg, lbound, tm, seq):
    n, d = x2d.shape
    assert w.shape[1] == 3 * ATT_W + 4 * HG_W and ATT_W == HG_W and seq % tm == 0 and tm % 256 == 0
    tok = lambda width: pl.BlockSpec((tm, width), lambda i: (i, 0))
    hg = lambda dt: jax.ShapeDtypeStruct((n, HG_W), dt)
    n_op = 3 * ATT_W // LANES
    sorted_by = lambda dil: (
        jax.ShapeDtypeStruct((n_op, n // tm, dil, tm // dil, LANES), BF16),
        pl.BlockSpec((n_op, None, dil, tm // dil, LANES), lambda i: (0, i, 0, 0, 0)))
    (shape1, spec1), (shape4, spec4), (shape16, spec16) = sorted_by(1), sorted_by(4), sorted_by(16)
    stage = pltpu.VMEM((2, tm, LANES), F32)
    return pl.pallas_call(
        _in_proj_kernel,
        out_shape=(shape1, shape4, shape16, hg(BF16), hg(F32), hg(BF16), hg(BF16), hg(BF16)),
        grid=(n // tm,),
        in_specs=[tok(d), _resident((1, d)), _resident(w.shape),
                  _resident((1, ATT_W)), _resident((1, ATT_W)), _resident((1, HG_W))],
        out_specs=(spec1, spec4, spec16, tok(HG_W), tok(HG_W), tok(HG_W), tok(HG_W), tok(HG_W)),
        scratch_shapes=[stage, stage],
        compiler_params=_params("parallel"),
        name="in_proj",
    )(x2d, g, w, qg, kg, lbound)


def _attn_bias_tables():
    slopes = 2.0 ** (-8.0 / ATT_HEADS * np.arange(1, ATT_HEADS + 1))
    qi = np.arange(BLK)[:, None]
    kj = np.arange(2 * BLK)[None, :]
    step = qi + BLK - kj
    tables = []
    for window, dilation in DILATED_PATTERNS:
        steps = window // dilation
        assert steps == BLK
        valid = (step >= 0) & (step <= steps)
        bias = -slopes[:, None, None] * (step * dilation).astype(np.float64)[None] * LOG2E
        t = np.where(valid[None], bias, NEG)
        tables.append(t.reshape(ATT_HEADS // 2, 2 * BLK, 2 * BLK))
    return jnp.asarray(np.stack(tables), F32)


def _attn_kernel(q1, k1, v1, q4, k4, v4, q16, k16, v16, bias_ref, o_ref,
                 acc4, m4, l4, acc1, m1, l1):
    seq = o_ref.shape[0]
    qs, ks, vs = (q1, q4, q16), (k1, k4, k16), (v1, v4, v16)
    assert [d for _, d in DILATED_PATTERNS] == [1, 4, 16] and seq == 16 * BLK
    lane = lax.broadcasted_iota(jnp.int32, (1, LANES), 1)
    first = lane < ATT_HEAD_DIM
    quarter = seq // 4
    zero = jnp.zeros((), BF16)
    ones = jnp.ones((2 * BLK, LANES), BF16)

    def sorted_rows(ref, blk, with_prev):
        _, dil, tile_len, _ = ref.shape
        per_sub = seq // dil // BLK
        res, n = blk // per_sub, blk % per_sub
        lo, hi = (n - 1 if with_prev else n) * BLK, (n + 1) * BLK
        parts = []
        while lo < hi:
            off = lo % tile_len
            take = min(hi - lo, tile_len - off)
            parts.append(ref[lo // tile_len, res, off:off + take, :])
            lo += take
        return parts[0] if len(parts) == 1 else jnp.concatenate(parts, axis=0)

    def scores(pi, blk, with_prev):
        q = sorted_rows(qs[pi], blk, False)
        q2 = jnp.concatenate([jnp.where(first, q, zero), jnp.where(first, zero, q)], axis=0)
        bias = bias_ref[pi] if with_prev else bias_ref[pi, :, BLK:]
        return _dot_nt(q2, sorted_rows(ks[pi], blk, with_prev)) + bias

    def block(pi, blk, with_prev, s):
        v = sorted_rows(vs[pi], blk, with_prev)
        v2 = jnp.concatenate([v, ones[:v.shape[0]]], axis=-1)
        m = jnp.max(s, axis=-1, keepdims=True)
        p = jnp.exp2(s - m).astype(BF16)
        r = _dot(p, v2)
        acc = jnp.where(first, r[:BLK, :LANES], r[BLK:, :LANES])
        l = jnp.where(first, r[:BLK, LANES:], r[BLK:, LANES:])
        mm = jnp.where(first, m[:BLK], m[BLK:])
        return acc, mm, l

    def merge(a, b):
        m = jnp.maximum(a[1], b[1])
        wa = jnp.exp2(a[1] - m)
        wb = jnp.exp2(b[1] - m)
        return wa * a[0] + wb * b[0], m, wa * a[2] + wb * b[2]

    n_groups = seq // BLK // ATTN_GROUP
    assert n_groups == 1

    def grouped(pi, has_prev, consume, groups=(0, n_groups)):
        def group_body(g, carry):
            s_next = scores(pi, g * ATTN_GROUP, has_prev(0))
            for j in range(ATTN_GROUP):
                s_cur = s_next
                if j + 1 < ATTN_GROUP:
                    s_next = scores(pi, g * ATTN_GROUP + j + 1, has_prev(j + 1))
                consume(g, j, block(pi, g * ATTN_GROUP + j, has_prev(j), s_cur))
            return carry

        if groups[1] - groups[0] == 1:
            group_body(groups[0], 0)
        elif groups[1] > groups[0]:
            lax.fori_loop(groups[0], groups[1], group_body, 0)

    def p16(g, j, res):
        a = g * (ATTN_GROUP // 4) + j // 4
        dst = pl.ds((j % 4) * quarter + a, BLK, stride=4)
        acc4[dst, :], m4[dst, :], l4[dst, :] = res

    grouped(2, lambda j: False, p16)

    def p4(g, j, res):
        blk = g * ATTN_GROUP + j
        r4 = g * (ATTN_GROUP // 4) + j // 4
        src = pl.ds(pl.multiple_of(blk * BLK, BLK), BLK)
        dst = pl.ds(r4 + (j % 4) * (4 * BLK), BLK, stride=4)
        acc1[dst, :], m1[dst, :], l1[dst, :] = merge(res, (acc4[src, :], m4[src, :], l4[src, :]))

    grouped(1, lambda j: j % 4 > 0, p4)

    def p1(g, j, res):
        src = pl.ds(pl.multiple_of((g * ATTN_GROUP + j) * BLK, BLK), BLK)
        acc, _, l = merge(res, (acc1[src, :], m1[src, :], l1[src, :]))
        o_ref[src, :] = (acc / l).astype(o_ref.dtype)

    grouped(0, lambda j: j > 0, p1, groups=(0, 1))
    grouped(0, lambda j: True, p1, groups=(1, n_groups))


def _dilated_attention(operands, bias, s):
    b = operands[0].shape[1]
    pairs = ATT_W // LANES
    n_pat = len(DILATED_PATTERNS)
    seq_f32 = pltpu.VMEM((s, LANES), F32)

    def qkv_specs(op):
        inner = op.shape[2:]
        spec = lambda off: pl.BlockSpec((None, None) + inner,
                                        lambda bi, hp: (off + hp, bi) + (0,) * len(inner))
        return [spec(0), spec(pairs), spec(2 * pairs)]

    return pl.pallas_call(
        _attn_kernel,
        out_shape=jax.ShapeDtypeStruct((pairs, b, s, LANES), BF16),
        grid=(b, pairs),
        in_specs=[spec for op in operands for spec in qkv_specs(op)]
        + [pl.BlockSpec((n_pat, None, 2 * BLK, 2 * BLK), lambda bi, hp: (0, hp, 0, 0))],
        out_specs=pl.BlockSpec((None, None, s, LANES), lambda bi, hp: (hp, bi, 0, 0)),
        scratch_shapes=[seq_f32] * 6,
        compiler_params=_params("parallel", "parallel"),
        name="dilated_attn",
    )(*[op for op in operands for _ in range(3)], bias)


def _hgrn_tables():
    t = np.arange(HG_CHUNK)[:, None]
    s = np.arange(HG_CHUNK)[None, :]
    ltri = (s <= t).astype(np.float32)
    masks, signs = [], []
    for m in HG_LEVELS:
        same_pair = (t // (2 * m)) == (s // (2 * m))
        masks.append((same_pair & (t % (2 * m) >= m) & (s % (2 * m) < m)).astype(np.float32))
        signs.append(np.broadcast_to(np.where(t % (2 * m) < m, 1.0, -1.0), (HG_CHUNK, LANES)))
    signs = np.stack(signs)
    return (jnp.asarray(ltri, BF16), jnp.asarray(np.stack(masks), F32),
            jnp.asarray(signs, F32), jnp.asarray(signs, BF16))


def _pair_reference(b, m):
    c = HG_CHUNK
    if m >= SUBLANES:
        pieces = []
        for p in range(c // (2 * m)):
            i0 = 2 * m * p + m
            pieces.append(jnp.broadcast_to(b[i0:i0 + 1, :], (2 * m, LANES)))
        return jnp.concatenate(pieces, axis=0)
    groups = c // SUBLANES
    b3 = b.reshape(groups, SUBLANES, LANES)
    sub = lax.broadcasted_iota(jnp.int32, (groups, SUBLANES, LANES), 1)
    ref = None
    for p in range(SUBLANES // (2 * m)):
        i0 = 2 * m * p + m
        piece = jnp.broadcast_to(b3[:, i0:i0 + 1, :], (groups, SUBLANES, LANES))
        ref = piece if ref is None else jnp.where(sub >= 2 * m * p, piece, ref)
    return ref.reshape(c, LANES)


def _hgrn_group(q, lf, kk, vb, states, ltri, lmask_ref, lsign_ref, lsign16_ref):
    heads = range(len(q))
    chunks = range(len(q[0]))
    grid = lambda fn: [[fn(h, c) for c in chunks] for h in heads]

    hi = grid(lambda h, c: lf[h][c].astype(BF16))
    lo = grid(lambda h, c: (lf[h][c] - hi[h][c].astype(F32)).astype(BF16))
    b = grid(lambda h, c: _dot(ltri, hi[h][c]) + _dot(ltri, lo[h][c]))

    a = grid(lambda h, c: jnp.zeros((HG_CHUNK, HG_CHUNK), F32))
    for li, m in enumerate(HG_LEVELS):
        def level_operand(h, c):
            e = jnp.exp2((_pair_reference(b[h][c], m) - b[h][c]) * lsign_ref[li])
            return jnp.where(lsign16_ref[li] > 0, kk[h][c], q[h][c]) * e.astype(BF16)

        xs = grid(level_operand)
        p = grid(lambda h, c: _dot_nt(xs[h][c], xs[h][c]))
        a = grid(lambda h, c: a[h][c] + lmask_ref[li] * p[h][c])
    o = grid(lambda h, c: _dot(a[h][c].astype(BF16), vb[h][c]))
    diag = grid(lambda h, c: jnp.sum((q[h][c] * kk[h][c]).astype(F32), axis=-1, keepdims=True))
    o = grid(lambda h, c: o[h][c] + diag[h][c] * vb[h][c].astype(F32))

    b_last = grid(lambda h, c: b[h][c][HG_CHUNK - 1:HG_CHUNK, :])
    k_end = grid(lambda h, c: kk[h][c] * jnp.exp2(b_last[h][c] - b[h][c]).astype(BF16))
    own = grid(lambda h, c: _dot_tn(vb[h][c], k_end[h][c]))
    carried = []
    new_states = []
    for h in heads:
        st = states[h]
        row = []
        for c in chunks:
            row.append(st)
            st = st * jnp.exp2(b_last[h][c]) + own[h][c]
        carried.append(row)
        new_states.append(st)
    qd = grid(lambda h, c: q[h][c] * jnp.exp2(b[h][c]).astype(BF16))
    o = grid(lambda h, c: o[h][c] + _dot_nt(qd[h][c], carried[h][c].astype(BF16)))
    return o, new_states


def _hgrn_kernel(q_ref, lf_ref, kk_ref, i_ref, gs_ref, og_ref, ltri_ref, lmask_ref, lsign_ref,
                 lsign16_ref, o_ref, st_ref):
    @pl.when(pl.program_id(1) == 0)
    def _():
        st_ref[...] = jnp.zeros_like(st_ref)

    n_heads = q_ref.shape[1] // LANES
    n_groups = q_ref.shape[0] // (HG_CHUNK * HG_GROUP)
    ltri = ltri_ref[...]

    def group_body(gi, carry):
        def rows(c):
            return pl.ds(pl.multiple_of((gi * HG_GROUP + c) * HG_CHUNK, HG_CHUNK), HG_CHUNK)

        cols = [slice(h * LANES, (h + 1) * LANES) for h in range(n_heads)]
        load = lambda ref: [[ref[rows(c), cs] for c in range(HG_GROUP)] for cs in cols]
        o, states = _hgrn_group(load(q_ref), load(lf_ref), load(kk_ref), load(i_ref),
                                [st_ref[h] for h in range(n_heads)],
                                ltri, lmask_ref, lsign_ref, lsign16_ref)
        for h, cs in enumerate(cols):
            st_ref[h] = states[h]
            for c in range(HG_GROUP):
                o_ref[rows(c), cs] = _rms(o[h][c], og_ref[:, cs]).astype(BF16) * gs_ref[rows(c), cs]
        return carry

    lax.fori_loop(0, n_groups, group_body, 0)


def _hgrn(hq, lf, kk, iv, gs, onorm_g, tables, ts):
    b, s, _ = hq.shape
    col = pl.BlockSpec((None, ts, HG_W), lambda bi, si: (bi, si, 0))
    return pl.pallas_call(
        _hgrn_kernel,
        out_shape=jax.ShapeDtypeStruct((b, s, HG_W), BF16),
        grid=(b, s // ts),
        in_specs=[col, col, col, col, col, _resident((1, HG_W))]
        + [_resident(t.shape) for t in tables],
        out_specs=col,
        scratch_shapes=[pltpu.VMEM((HG_HEADS, HG_HEAD_DIM, HG_HEAD_DIM), F32)],
        compiler_params=_params("parallel", "arbitrary"),
        name="hgrn2",
    )(hq, lf, kk, iv, gs, onorm_g, *tables)


def _mem_kv_kernel(mem_ref, g_ref, w_ref, kg_ref, km_ref, vm_ref):
    d = mem_ref.shape[-1]
    hd = d // MEM_HEADS
    mn = _rms(mem_ref[...], g_ref[...]).astype(BF16)
    for h in range(MEM_HEADS):
        k = _dot(mn, w_ref[:, h * hd:(h + 1) * hd])
        km_ref[:, h * hd:(h + 1) * hd] = _rms(k, kg_ref[...]).astype(BF16)
    vm_ref[...] = _dot(mn, w_ref[:, d:]).astype(BF16)


def _mem_kv(mem, g, w, kg):
    b, n_mem, d = mem.shape
    blk = pl.BlockSpec((None, n_mem, d), lambda bi: (bi, 0, 0))
    return pl.pallas_call(
        _mem_kv_kernel,
        out_shape=(jax.ShapeDtypeStruct((b, n_mem, d), BF16),) * 2,
        grid=(b,),
        in_specs=[blk, _resident((1, d)), _resident(w.shape), _resident((1, d // MEM_HEADS))],
        out_specs=(blk, blk),
        compiler_params=_params("parallel"),
        name="mem_kv",
    )(mem, g, w, kg)


def _mix_mem_kernel(x_ref, ya_ref, yh_ref, wo_ref, g_ref, wq_ref, qg_ref, km_ref, vm_ref, wmo_ref,
                    o_ref):
    tm, d = x_ref.shape
    hd = d // MEM_HEADS
    scale = 1.0 / math.sqrt(hd)
    rows = tm // MIX_SLABS
    slabs = [slice(part * rows, (part + 1) * rows) for part in range(MIX_SLABS)]
    each = lambda fn, *lists: [fn(*args) for args in zip(*lists)]

    def mixed(r):
        ya = jnp.concatenate([ya_ref[p, r, :] for p in range(ya_ref.shape[0])], axis=-1)
        return _dot(ya, wo_ref[:ATT_W, :]) + _dot(yh_ref[r, :], wo_ref[ATT_W:, :])

    y = each(mixed, slabs)
    x1 = each(lambda r, yy: x_ref[r, :] + yy, slabs, y)
    h = each(lambda xx: _rms(xx, g_ref[...]).astype(BF16), x1)
    heads = [[] for _ in slabs]
    for hi in range(MEM_HEADS):
        cs = slice(hi * hd, (hi + 1) * hd)
        q = each(lambda hh: _dot(hh, wq_ref[:, cs]), h)
        qh = each(lambda qq: (_rms(qq, qg_ref[...]) * scale).astype(BF16), q)
        s = each(lambda qq: _dot_nt(qq, km_ref[:, cs]), qh)
        m = each(lambda ss: jnp.max(ss, axis=-1, keepdims=True), s)
        p = each(lambda ss, mm: jnp.exp(ss - mm), s, m)
        l = each(lambda pp: jnp.sum(pp, axis=-1, keepdims=True), p)
        o = each(lambda pp: _dot(pp.astype(BF16), vm_ref[:, cs]), p)
        for part, (oo, ll) in enumerate(zip(o, l)):
            heads[part].append((oo / ll).astype(BF16))
    om = each(lambda hs: jnp.concatenate(hs, axis=-1), heads)
    out = each(lambda xx, oo: xx + _dot(oo, wmo_ref[...]), x1, om)
    for r, oo in zip(slabs, out):
        o_ref[r, :] = oo


def _mix_mem(x2d, ya, yh, wo, g, wq, qg, km, vm, wmo, tm, seq):
    n, d = x2d.shape
    n_mem = km.shape[1]
    tiles_per_seq = seq // tm
    tok = lambda w: pl.BlockSpec((tm, w), lambda i: (i, 0))
    mem_blk = pl.BlockSpec((None, n_mem, d), lambda i: (i // tiles_per_seq, 0, 0))
    return pl.pallas_call(
        _mix_mem_kernel,
        out_shape=jax.ShapeDtypeStruct((n, d), F32),
        grid=(n // tm,),
        in_specs=[tok(d), pl.BlockSpec((ya.shape[0], tm, LANES), lambda i: (0, i, 0)), tok(HG_W),
                  _resident(wo.shape), _resident((1, d)),
                  _resident(wq.shape), _resident((1, d // MEM_HEADS)), mem_blk, mem_blk,
                  _resident(wmo.shape)],
        out_specs=tok(d),
        compiler_params=_params("parallel"),
        name="mix_mem",
    )(x2d, ya, yh, wo, g, wq, qg, km, vm, wmo)


def _ffn_kernel(x_ref, g_ref, w1_ref, w2_ref, o_ref):
    x = x_ref[...]
    h = _rms(x, g_ref[...]).astype(BF16)
    d_ff = w1_ref.shape[1]
    step = 1024
    acc = x
    for c in range(0, d_ff, step):
        u = jnp.maximum(_dot(h, w1_ref[:, c:c + step]), 0.0)
        acc = acc + _dot((u * u).astype(BF16), w2_ref[c:c + step, :])
    o_ref[...] = acc


def _ffn(x2d, g, w1, w2, tm):
    n, d = x2d.shape
    tok = pl.BlockSpec((tm, d), lambda i: (i, 0))
    return pl.pallas_call(
        _ffn_kernel,
        out_shape=jax.ShapeDtypeStruct((n, d), F32),
        grid=(n // tm,),
        in_specs=[tok, _resident((1, d)), _resident(w1.shape), _resident(w2.shape)],
        out_specs=tok,
        compiler_params=_params("parallel"),
        name="ffn",
    )(x2d, g, w1, w2)


def kernel(x, mem, norm1_g, w_in, attn_qn_g, attn_kn_g, hg_lb, hg_onorm_g, w_out, norm2_g,
           mem_norm_g, w_mq, w_mkv, mq_norm_g, mk_norm_g, w_mo, norm3_g, w_ff1, w_ff2):
    bsz, seq, d = x.shape
    depth = w_in.shape[0]
    tm = 512
    assert seq % tm == 0 and seq % HG_CHUNK == 0

    p_lb = jax.nn.softmax(hg_lb.astype(F32), axis=0)
    lower_bounds = jnp.cumsum(p_lb, axis=0) - p_lb[0:1]
    bias = _attn_bias_tables()
    hg_tables = _hgrn_tables()
    row = lambda v: v.reshape(1, -1).astype(F32)

    x2d = x.reshape(bsz * seq, d)
    for l in range(depth):
        qg = jnp.tile(attn_qn_g[l].astype(F32), ATT_HEADS) * (LOG2E / math.sqrt(ATT_HEAD_DIM))
        kg = jnp.tile(attn_kn_g[l].astype(F32), ATT_HEADS)
        streams = _in_proj(x2d, row(norm1_g[l]), w_in[l].astype(BF16), row(qg), row(kg),
                           row(lower_bounds[l]), 2 * tm, seq)
        operands = [t.reshape(t.shape[0], bsz, -1, *t.shape[2:]) for t in streams[:3]]
        hq, lf, kk, iv, gs = (t.reshape(bsz, seq, HG_W) for t in streams[3:])
        ya = _dilated_attention(operands, bias, seq)
        yh = _hgrn(hq, lf, kk, iv, gs, row(hg_onorm_g[l]), hg_tables, 2 * tm)
        km, vm = _mem_kv(mem, row(mem_norm_g[l]), w_mkv[l].astype(BF16), row(mk_norm_g[l]))
        x2d = _mix_mem(x2d, ya.reshape(-1, bsz * seq, LANES), yh.reshape(bsz * seq, HG_W),
                       w_out[l].astype(BF16), row(norm2_g[l]), w_mq[l].astype(BF16),
                       row(mq_norm_g[l]), km, vm, w_mo[l].astype(BF16), tm * MIX_SLABS, seq)
        x2d = _ffn(x2d, row(norm3_g[l]), w_ff1[l].astype(BF16), w_ff2[l].astype(BF16), 2 * tm)
    return x2d.reshape(bsz, seq, d)
```

```python
import math

import jax
import jax.numpy as jnp
import numpy as np
from jax import lax
from jax.experimental import pallas as pl
from jax.experimental.pallas import tpu as pltpu

F32 = jnp.float32
BF16 = jnp.bfloat16

EPS = 1e-6
NEG = -1e30
F_MIN = 1e-12
LOG2E = math.log2(math.e)

ATT_HEADS = 8
ATT_HEAD_DIM = 64
ATT_W = ATT_HEADS * ATT_HEAD_DIM
DILATED_PATTERNS = ((128, 1), (512, 4), (2048, 16))
BLK = 128
HG_HEADS = 4
HG_HEAD_DIM = 128
HG_W = HG_HEADS * HG_HEAD_DIM
MEM_HEADS = 4

LANES = 128
SUBLANES = 8
V7X_VMEM_BYTES = 64 * 1024 * 1024
VMEM_LIMIT = V7X_VMEM_BYTES - 8 * 1024 * 1024

IN_PIECE = 256
ATTN_GROUP = 16
MIX_SLABS = 2
HG_CHUNK = 128
HG_GROUP = 2
HG_LEVELS = tuple(1 << i for i in range(int(math.log2(HG_CHUNK))))


def _rms(x, g):
    return x * lax.rsqrt(jnp.mean(x * x, axis=-1, keepdims=True) + EPS) * g


def _sigmoid(x):
    return 1.0 / (1.0 + jnp.exp(-x))


def _dot(a, b):
    return jnp.dot(a, b, preferred_element_type=F32)


def _dot_nt(a, b):
    return lax.dot_general(a, b, (((1,), (1,)), ((), ())), preferred_element_type=F32)


def _dot_tn(a, b):
    return lax.dot_general(a, b, (((0,), (0,)), ((), ())), preferred_element_type=F32)


def _params(*semantics):
    return pltpu.CompilerParams(dimension_semantics=semantics, vmem_limit_bytes=VMEM_LIMIT)


def _resident(shape):
    nd = len(shape)
    return pl.BlockSpec(shape, lambda *_: (0,) * nd, pipeline_mode=pl.Buffered(1))


def _in_proj_kernel(x_ref, g_ref, w_ref, qg_ref, kg_ref, lb_ref,
                    s1_ref, s4_ref, s16_ref, hq_ref, lf_ref, kk_ref, iv_ref, gs_ref,
                    stage_ref, stage4_ref):
    h = _rms(x_ref[...], g_ref[...]).astype(BF16)
    lane = lax.broadcasted_iota(jnp.int32, (1, LANES), 1)
    first = lane < ATT_HEAD_DIM
    pairs = ATT_W // LANES
    tm = x_ref.shape[0]
    assert [d for _, d in DILATED_PATTERNS] == [1, 4, 16]

    def emit(idx, x):
        slot = idx % 2
        s1_ref[idx, 0] = x.astype(BF16)
        stage_ref[slot] = x
        for r in range(4):
            y = stage_ref[slot, pl.ds(r, tm // 4, stride=4), :]
            s4_ref[idx, r] = y.astype(BF16)
            stage4_ref[slot, r * (tm // 4):(r + 1) * (tm // 4), :] = y
        for a in range(4):
            for r in range(4):
                y = stage4_ref[slot, pl.ds(r * (tm // 4) + a, tm // 16, stride=4), :]
                s16_ref[idx, 4 * a + r] = y.astype(BF16)

    def pieces(c):
        for j in range(ATT_W // IN_PIECE):
            cs = slice(j * IN_PIECE, (j + 1) * IN_PIECE)
            yield cs, _dot(h, w_ref[:, c * ATT_W + j * IN_PIECE:c * ATT_W + (j + 1) * IN_PIECE])

    def attention_operand(c, gain_ref=None):
        for cs, z in pieces(c):
            for j in range(IN_PIECE // LANES):
                lo = cs.start + j * LANES
                x = z[:, j * LANES:(j + 1) * LANES]
                if gain_ref is not None:
                    x2 = x * x
                    sa = jnp.sum(jnp.where(first, x2, 0.0), axis=-1, keepdims=True)
                    sb = jnp.sum(jnp.where(first, 0.0, x2), axis=-1, keepdims=True)
                    ms = jnp.where(first, sa, sb) * (1.0 / ATT_HEAD_DIM)
                    x = x * lax.rsqrt(ms + EPS) * gain_ref[:, lo:lo + LANES]
                emit(c * pairs + lo // LANES, x)

    attention_operand(0, qg_ref)
    attention_operand(1, kg_ref)
    attention_operand(2)
    for cs, z in pieces(4):
        lbound = lb_ref[:, cs]
        sg = _sigmoid(z)
        f = lbound + (1.0 - lbound) * sg
        lf_ref[:, cs] = jnp.log2(jnp.maximum(f, F_MIN))
        kk_ref[:, cs] = ((1.0 - lbound) * (1.0 - sg)).astype(BF16)
    for cs, z in pieces(3):
        hq_ref[:, cs] = (z * _sigmoid(z)).astype(BF16)
    for cs, z in pieces(6):
        gs_ref[:, cs] = (z * _sigmoid(z)).astype(BF16)
    for cs, z in pieces(5):
        iv_ref[:, cs] = z.astype(BF16)


def _in_proj(x2d, g, w, qg, kg, lbound, tm, seq):
    n, d = x2d.shape
    assert w.shape[1] == 3 * ATT_W + 4 * HG_W and ATT_W == HG_W and seq % tm == 0 and tm % 256 == 0
    tok = lambda width: pl.BlockSpec((tm, width), lambda i: (i, 0))
    hg = lambda dt: jax.ShapeDtypeStruct((n, HG_W), dt)
    n_op = 3 * ATT_W // LANES
    sorted_by = lambda dil: (
        jax.ShapeDtypeStruct((n_op, n // tm, dil, tm // dil, LANES), BF16),
        pl.BlockSpec((n_op, None, dil, tm // dil, LANES), lambda i: (0, i, 0, 0, 0)))
    (shape1, spec1), (shape4, spec4), (shape16, spec16) = sorted_by(1), sorted_by(4), sorted_by(16)
    stage = pltpu.VMEM((2, tm, LANES), F32)
    return pl.pallas_call(
        _in_proj_kernel,
        out_shape=(shape1, shape4, shape16, hg(BF16), hg(F32), hg(BF16), hg(BF16), hg(BF16)),
        grid=(n // tm,),
        in_specs=[tok(d), _resident((1, d)), _resident(w.shape),
                  _resident((1, ATT_W)), _resident((1, ATT_W)), _resident((1, HG_W))],
        out_specs=(spec1, spec4, spec16, tok(HG_W), tok(HG_W), tok(HG_W), tok(HG_W), tok(HG_W)),
        scratch_shapes=[stage, stage],
        compiler_params=_params("parallel"),
        name="in_proj",
    )(x2d, g, w, qg, kg, lbound)


def _attn_bias_tables():
    slopes = 2.0 ** (-8.0 / ATT_HEADS * np.arange(1, ATT_HEADS + 1))
    qi = np.arange(BLK)[:, None]
    kj = np.arange(2 * BLK)[None, :]
    step = qi + BLK - kj
    tables = []
    for window, dilation in DILATED_PATTERNS:
        steps = window // dilation
        assert steps == BLK
        valid = (step >= 0) & (step <= steps)
        bias = -slopes[:, None, None] * (step * dilation).astype(np.float64)[None] * LOG2E
        t = np.where(valid[None], bias, NEG)
        tables.append(t.reshape(ATT_HEADS // 2, 2 * BLK, 2 * BLK))
    return jnp.asarray(np.stack(tables), F32)


def _attn_kernel(q1, k1, v1, q4, k4, v4, q16, k16, v16, bias_ref, o_ref,
                 acc4, m4, l4, acc1, m1, l1):
    seq = o_ref.shape[0]
    qs, ks, vs = (q1, q4, q16), (k1, k4, k16), (v1, v4, v16)
    assert [d for _, d in DILATED_PATTERNS] == [1, 4, 16] and seq == 16 * BLK
    lane = lax.broadcasted_iota(jnp.int32, (1, LANES), 1)
    first = lane < ATT_HEAD_DIM
    quarter = seq // 4
    zero = jnp.zeros((), BF16)
    ones = jnp.ones((2 * BLK, LANES), BF16)

    def sorted_rows(ref, blk, with_prev):
        _, dil, tile_len, _ = ref.shape
        per_sub = seq // dil // BLK
        res, n = blk // per_sub, blk % per_sub
        lo, hi = (n - 1 if with_prev else n) * BLK, (n + 1) * BLK
        parts = []
        while lo < hi:
            off = lo % tile_len
            take = min(hi - lo, tile_len - off)
            parts.append(ref[lo // tile_len, res, off:off + take, :])
            lo += take
        return parts[0] if len(parts) == 1 else jnp.concatenate(parts, axis=0)

    def scores(pi, blk, with_prev):
        q = sorted_rows(qs[pi], blk, False)
        q2 = jnp.concatenate([jnp.where(first, q, zero), jnp.where(first, zero, q)], axis=0)
        bias = bias_ref[pi] if with_prev else bias_ref[pi, :, BLK:]
        return _dot_nt(q2, sorted_rows(ks[pi], blk, with_prev)) + bias

    def block(pi, blk, with_prev, s):
        v = sorted_rows(vs[pi], blk, with_prev)
        v2 = jnp.concatenate([v, ones[:v.shape[0]]], axis=-1)
        m = jnp.max(s, axis=-1, keepdims=True)
        p = jnp.exp2(s - m).astype(BF16)
        r = _dot(p, v2)
        acc = jnp.where(first, r[:BLK, :LANES], r[BLK:, :LANES])
        l = jnp.where(first, r[:BLK, LANES:], r[BLK:, LANES:])
        mm = jnp.where(first, m[:BLK], m[BLK:])
        return acc, mm, l

    def merge(a, b):
        m = jnp.maximum(a[1], b[1])
        wa = jnp.exp2(a[1] - m)
        wb = jnp.exp2(b[1] - m)
        return wa * a[0] + wb * b[0], m, wa * a[2] + wb * b[2]

    n_groups = seq // BLK // ATTN_GROUP
    assert n_groups == 1

    def grouped(pi, has_prev, consume, groups=(0, n_groups)):
        def group_body(g, carry):
            s_next = scores(pi, g * ATTN_GROUP, has_prev(0))
            for j in range(ATTN_GROUP):
                s_cur = s_next
                if j + 1 < ATTN_GROUP:
                    s_next = scores(pi, g * ATTN_GROUP + j + 1, has_prev(j + 1))
                consume(g, j, block(pi, g * ATTN_GROUP + j, has_prev(j), s_cur))
            return carry

        if groups[1] - groups[0] == 1:
            group_body(groups[0], 0)
        elif groups[1] > groups[0]:
            lax.fori_loop(groups[0], groups[1], group_body, 0)

    def p16(g, j, res):
        a = g * (ATTN_GROUP // 4) + j // 4
        dst = pl.ds((j % 4) * quarter + a, BLK, stride=4)
        acc4[dst, :], m4[dst, :], l4[dst, :] = res

    grouped(2, lambda j: False, p16)

    def p4(g, j, res):
        blk = g * ATTN_GROUP + j
        r4 = g * (ATTN_GROUP // 4) + j // 4
        src = pl.ds(pl.multiple_of(blk * BLK, BLK), BLK)
        dst = pl.ds(r4 + (j % 4) * (4 * BLK), BLK, stride=4)
        acc1[dst, :], m1[dst, :], l1[dst, :] = merge(res, (acc4[src, :], m4[src, :], l4[src, :]))

    grouped(1, lambda j: j % 4 > 0, p4)

    def p1(g, j, res):
        src = pl.ds(pl.multiple_of((g * ATTN_GROUP + j) * BLK, BLK), BLK)
        acc, _, l = merge(res, (acc1[src, :], m1[src, :], l1[src, :]))
        o_ref[src, :] = (acc / l).astype(o_ref.dtype)

    grouped(0, lambda j: j > 0, p1, groups=(0, 1))
    grouped(0, lambda j: True, p1, groups=(1, n_groups))


def _dilated_attention(operands, bias, s):
    b = operands[0].shape[1]
    pairs = ATT_W // LANES
    n_pat = len(DILATED_PATTERNS)
    seq_f32 = pltpu.VMEM((s, LANES), F32)

    def qkv_specs(op):
        inner = op.shape[2:]
        spec = lambda off: pl.BlockSpec((None, None) + inner,
                                        lambda bi, hp: (off + hp, bi) + (0,) * len(inner))
        return [spec(0), spec(pairs), spec(2 * pairs)]

    return pl.pallas_call(
        _attn_kernel,
        out_shape=jax.ShapeDtypeStruct((pairs, b, s, LANES), BF16),
        grid=(b, pairs),
        in_specs=[spec for op in operands for spec in qkv_specs(op)]
        + [pl.BlockSpec((n_pat, None, 2 * BLK, 2 * BLK), lambda bi, hp: (0, hp, 0, 0))],
        out_specs=pl.BlockSpec((None, None, s, LANES), lambda bi, hp: (hp, bi, 0, 0)),
        scratch_shapes=[seq_f32] * 6,
        compiler_params=_params("parallel", "parallel"),
        name="dilated_attn",
    )(*[op for op in operands for _ in range(3)], bias)


def _hgrn_tables():
    t = np.arange(HG_CHUNK)[:, None]
    s = np.arange(HG_CHUNK)[None, :]
    ltri = (s <= t).astype(np.float32)
    masks, signs = [], []
    for m in HG_LEVELS:
        same_pair = (t // (2 * m)) == (s // (2 * m))
        masks.append((same_pair & (t % (2 * m) >= m) & (s % (2 * m) < m)).astype(np.float32))
        signs.append(np.broadcast_to(np.where(t % (2 * m) < m, 1.0, -1.0), (HG_CHUNK, LANES)))
    signs = np.stack(signs)
    return (jnp.asarray(ltri, BF16), jnp.asarray(np.stack(masks), F32),
            jnp.asarray(signs, F32), jnp.asarray(signs, BF16))


def _pair_reference(b, m):
    c = HG_CHUNK
    if m >= SUBLANES:
        pieces = []
        for p in range(c // (2 * m)):
            i0 = 2 * m * p + m
            pieces.append(jnp.broadcast_to(b[i0:i0 + 1, :], (2 * m, LANES)))
        return jnp.concatenate(pieces, axis=0)
    groups = c // SUBLANES
    b3 = b.reshape(groups, SUBLANES, LANES)
    sub = lax.broadcasted_iota(jnp.int32, (groups, SUBLANES, LANES), 1)
    ref = None
    for p in range(SUBLANES // (2 * m)):
        i0 = 2 * m * p + m
        piece = jnp.broadcast_to(b3[:, i0:i0 + 1, :], (groups, SUBLANES, LANES))
        ref = piece if ref is None else jnp.where(sub >= 2 * m * p, piece, ref)
    return ref.reshape(c, LANES)


def _hgrn_group(q, lf, kk, vb, states, ltri, lmask_ref, lsign_ref, lsign16_ref):
    heads = range(len(q))
    chunks = range(len(q[0]))
    grid = lambda fn: [[fn(h, c) for c in chunks] for h in heads]

    hi = grid(lambda h, c: lf[h][c].astype(BF16))
    lo = grid(lambda h, c: (lf[h][c] - hi[h][c].astype(F32)).astype(BF16))
    b = grid(lambda h, c: _dot(ltri, hi[h][c]) + _dot(ltri, lo[h][c]))

    a = grid(lambda h, c: jnp.zeros((HG_CHUNK, HG_CHUNK), F32))
    for li, m in enumerate(HG_LEVELS):
        def level_operand(h, c):
            e = jnp.exp2((_pair_reference(b[h][c], m) - b[h][c]) * lsign_ref[li])
            return jnp.where(lsign16_ref[li] > 0, kk[h][c], q[h][c]) * e.astype(BF16)

        xs = grid(level_operand)
        p = grid(lambda h, c: _dot_nt(xs[h][c], xs[h][c]))
        a = grid(lambda h, c: a[h][c] + lmask_ref[li] * p[h][c])
    o = grid(lambda h, c: _dot(a[h][c].astype(BF16), vb[h][c]))
    diag = grid(lambda h, c: jnp.sum((q[h][c] * kk[h][c]).astype(F32), axis=-1, keepdims=True))
    o = grid(lambda h, c: o[h][c] + diag[h][c] * vb[h][c].astype(F32))

    b_last = grid(lambda h, c: b[h][c][HG_CHUNK - 1:HG_CHUNK, :])
    k_end = grid(lambda h, c: kk[h][c] * jnp.exp2(b_last[h][c] - b[h][c]).astype(BF16))
    own = grid(lambda h, c: _dot_tn(vb[h][c], k_end[h][c]))
    carried = []
    new_states = []
    for h in heads:
        st = states[h]
        row = []
        for c in chunks:
            row.append(st)
            st = st * jnp.exp2(b_last[h][c]) + own[h][c]
        carried.append(row)
        new_states.append(st)
    qd = grid(lambda h, c: q[h][c] * jnp.exp2(b[h][c]).astype(BF16))
    o = grid(lambda h, c: o[h][c] + _dot_nt(qd[h][c], carried[h][c].astype(BF16)))
    return o, new_states


def _hgrn_kernel(q_ref, lf_ref, kk_ref, i_ref, gs_ref, og_ref, ltri_ref, lmask_ref, lsign_ref,
                 lsign16_ref, o_ref, st_ref):
    @pl.when(pl.program_id(1) == 0)
    def _():
        st_ref[...] = jnp.zeros_like(st_ref)

    n_heads = q_ref.shape[1] // LANES
    n_groups = q_ref.shape[0] // (HG_CHUNK * HG_GROUP)
    ltri = ltri_ref[...]

    def group_body(gi, carry):
        def rows(c):
            return pl.ds(pl.multiple_of((gi * HG_GROUP + c) * HG_CHUNK, HG_CHUNK), HG_CHUNK)

        cols = [slice(h * LANES, (h + 1) * LANES) for h in range(n_heads)]
        load = lambda ref: [[ref[rows(c), cs] for c in range(HG_GROUP)] for cs in cols]
        o, states = _hgrn_group(load(q_ref), load(lf_ref), load(kk_ref), load(i_ref),
                                [st_ref[h] for h in range(n_heads)],
                                ltri, lmask_ref, lsign_ref, lsign16_ref)
        for h, cs in enumerate(cols):
            st_ref[h] = states[h]
            for c in range(HG_GROUP):
                o_ref[rows(c), cs] = _rms(o[h][c], og_ref[:, cs]).astype(BF16) * gs_ref[rows(c), cs]
        return carry

    lax.fori_loop(0, n_groups, group_body, 0)


def _hgrn(hq, lf, kk, iv, gs, onorm_g, tables, ts):
    b, s, _ = hq.shape
    col = pl.BlockSpec((None, ts, HG_W), lambda bi, si: (bi, si, 0))
    return pl.pallas_call(
        _hgrn_kernel,
        out_shape=jax.ShapeDtypeStruct((b, s, HG_W), BF16),
        grid=(b, s // ts),
        in_specs=[col, col, col, col, col, _resident((1, HG_W))]
        + [_resident(t.shape) for t in tables],
        out_specs=col,
        scratch_shapes=[pltpu.VMEM((HG_HEADS, HG_HEAD_DIM, HG_HEAD_DIM), F32)],
        compiler_params=_params("parallel", "arbitrary"),
        name="hgrn2",
    )(hq, lf, kk, iv, gs, onorm_g, *tables)


def _mem_kv_kernel(mem_ref, g_ref, w_ref, kg_ref, km_ref, vm_ref):
    d = mem_ref.shape[-1]
    hd = d // MEM_HEADS
    mn = _rms(mem_ref[...], g_ref[...]).astype(BF16)
    for h in range(MEM_HEADS):
        k = _dot(mn, w_ref[:, h * hd:(h + 1) * hd])
        km_ref[:, h * hd:(h + 1) * hd] = _rms(k, kg_ref[...]).astype(BF16)
    vm_ref[...] = _dot(mn, w_ref[:, d:]).astype(BF16)


def _mem_kv(mem, g, w, kg):
    b, n_mem, d = mem.shape
    blk = pl.BlockSpec((None, n_mem, d), lambda bi: (bi, 0, 0))
    return pl.pallas_call(
        _mem_kv_kernel,
        out_shape=(jax.ShapeDtypeStruct((b, n_mem, d), BF16),) * 2,
        grid=(b,),
        in_specs=[blk, _resident((1, d)), _resident(w.shape), _resident((1, d // MEM_HEADS))],
        out_specs=(blk, blk),
        compiler_params=_params("parallel"),
        name="mem_kv",
    )(mem, g, w, kg)


def _mix_mem_kernel(x_ref, ya_ref, yh_ref, wo_ref, g_ref, wq_ref, qg_ref, km_ref, vm_ref, wmo_ref,
                    o_ref):
    tm, d = x_ref.shape
    hd = d // MEM_HEADS
    scale = 1.0 / math.sqrt(hd)
    rows = tm // MIX_SLABS
    slabs = [slice(part * rows, (part + 1) * rows) for part in range(MIX_SLABS)]
    each = lambda fn, *lists: [fn(*args) for args in zip(*lists)]

    def mixed(r):
        ya = jnp.concatenate([ya_ref[p, r, :] for p in range(ya_ref.shape[0])], axis=-1)
        return _dot(ya, wo_ref[:ATT_W, :]) + _dot(yh_ref[r, :], wo_ref[ATT_W:, :])

    y = each(mixed, slabs)
    x1 = each(lambda r, yy: x_ref[r, :] + yy, slabs, y)
    h = each(lambda xx: _rms(xx, g_ref[...]).astype(BF16), x1)
    heads = [[] for _ in slabs]
    for hi in range(MEM_HEADS):
        cs = slice(hi * hd, (hi + 1) * hd)
        q = each(lambda hh: _dot(hh, wq_ref[:, cs]), h)
        qh = each(lambda qq: (_rms(qq, qg_ref[...]) * scale).astype(BF16), q)
        s = each(lambda qq: _dot_nt(qq, km_ref[:, cs]), qh)
        m = each(lambda ss: jnp.max(ss, axis=-1, keepdims=True), s)
        p = each(lambda ss, mm: jnp.exp(ss - mm), s, m)
        l = each(lambda pp: jnp.sum(pp, axis=-1, keepdims=True), p)
        o = each(lambda pp: _dot(pp.astype(BF16), vm_ref[:, cs]), p)
        for part, (oo, ll) in enumerate(zip(o, l)):
            heads[part].append((oo / ll).astype(BF16))
    om = each(lambda hs: jnp.concatenate(hs, axis=-1), heads)
    out = each(lambda xx, oo: xx + _dot(oo, wmo_ref[...]), x1, om)
    for r, oo in zip(slabs, out):
        o_ref[r, :] = oo


def _mix_mem(x2d, ya, yh, wo, g, wq, qg, km, vm, wmo, tm, seq):
    n, d = x2d.shape
    n_mem = km.shape[1]
    tiles_per_seq = seq // tm
    tok = lambda w: pl.BlockSpec((tm, w), lambda i: (i, 0))
    mem_blk = pl.BlockSpec((None, n_mem, d), lambda i: (i // tiles_per_seq, 0, 0))
    return pl.pallas_call(
        _mix_mem_kernel,
        out_shape=jax.ShapeDtypeStruct((n, d), F32),
        grid=(n // tm,),
        in_specs=[tok(d), pl.BlockSpec((ya.shape[0], tm, LANES), lambda i: (0, i, 0)), tok(HG_W),
                  _resident(wo.shape), _resident((1, d)),
                  _resident(wq.shape), _resident((1, d // MEM_HEADS)), mem_blk, mem_blk,
                  _resident(wmo.shape)],
        out_specs=tok(d),
        compiler_params=_params("parallel"),
        name="mix_mem",
    )(x2d, ya, yh, wo, g, wq, qg, km, vm, wmo)


def _ffn_kernel(x_ref, g_ref, w1_ref, w2_ref, o_ref):
    x = x_ref[...]
    h = _rms(x, g_ref[...]).astype(BF16)
    d_ff = w1_ref.shape[1]
    step = 1024
    acc = x
    for c in range(0, d_ff, step):
        u = jnp.maximum(_dot(h, w1_ref[:, c:c + step]), 0.0)
        acc = acc + _dot((u * u).astype(BF16), w2_ref[c:c + step, :])
    o_ref[...] = acc


def _ffn(x2d, g, w1, w2, tm):
    n, d = x2d.shape
    tok = pl.BlockSpec((tm, d), lambda i: (i, 0))
    return pl.pallas_call(
        _ffn_kernel,
        out_shape=jax.ShapeDtypeStruct((n, d), F32),
        grid=(n // tm,),
        in_specs=[tok, _resident((1, d)), _resident(w1.shape), _resident(w2.shape)],
        out_specs=tok,
        compiler_params=_params("parallel"),
        name="ffn",
    )(x2d, g, w1, w2)


def kernel(x, mem, norm1_g, w_in, attn_qn_g, attn_kn_g, hg_lb, hg_onorm_g, w_out, norm2_g,
           mem_norm_g, w_mq, w_mkv, mq_norm_g, mk_norm_g, w_mo, norm3_g, w_ff1, w_ff2):
    bsz, seq, d = x.shape
    depth = w_in.shape[0]
    tm = 512
    assert seq % tm == 0 and seq % HG_CHUNK == 0

    p_lb = jax.nn.softmax(hg_lb.astype(F32), axis=0)
    lower_bounds = jnp.cumsum(p_lb, axis=0) - p_lb[0:1]
    bias = _attn_bias_tables()
    hg_tables = _hgrn_tables()
    row = lambda v: v.reshape(1, -1).astype(F32)

    x2d = x.reshape(bsz * seq, d)
    for l in range(depth):
        qg = jnp.tile(attn_qn_g[l].astype(F32), ATT_HEADS) * (LOG2E / math.sqrt(ATT_HEAD_DIM))
        kg = jnp.tile(attn_kn_g[l].astype(F32), ATT_HEADS)
        streams = _in_proj(x2d, row(norm1_g[l]), w_in[l].astype(BF16), row(qg), row(kg),
                           row(lower_bounds[l]), 2 * tm, seq)
        operands = [t.reshape(t.shape[0], bsz, -1, *t.shape[2:]) for t in streams[:3]]
        hq, lf, kk, iv, gs = (t.reshape(bsz, seq, HG_W) for t in streams[3:])
        ya = _dilated_attention(operands, bias, seq)
        yh = _hgrn(hq, lf, kk, iv, gs, row(hg_onorm_g[l]), hg_tables, seq)
        km, vm = _mem_kv(mem, row(mem_norm_g[l]), w_mkv[l].astype(BF16), row(mk_norm_g[l]))
        x2d = _mix_mem(x2d, ya.reshape(-1, bsz * seq, LANES), yh.reshape(bsz * seq, HG_W),
                       w_out[l].astype(BF16), row(norm2_g[l]), w_mq[l].astype(BF16),
                       row(mq_norm_g[l]), km, vm, w_mo[l].astype(BF16), tm * MIX_SLABS, seq)
        x2d = _ffn(x2d, row(norm3_g[l]), w_ff1[l].astype(BF16), w_ff2[l].astype(BF16), 2 * tm)
    return x2d.reshape(bsz, seq, d)
```
